```python
import math
import jax
import jax.numpy as jnp
from jax import lax
import numpy as np

D_MODEL = 1024
BATCH = 4
SEQ = 4096
DEPTH = 1
DEC_BATCH = 128
DEC_SEQ = 4
PAST_LEN = 8192
PAGE_SIZE = 128

H_A = 8
HD_A = 64
ATT_W = H_A * HD_A
MOBA_BLOCK = 256
MOBA_TOPK = 3
Q_BLOCK = 64
N_BUCKETS = 32
MAX_DISTANCE = 128
H_R = 4
DK_R = 128
DV_R = 128
RET_QK_W = H_R * DK_R
RET_V_W = H_R * DV_R
RET_CHUNK = 128
ROPE_BASE = 10000.0
N_EXPERTS = 256
TOP_K = 8
N_GROUPS = 8
TOPK_GROUPS = 4
D_EXPERT = 256
D_SHARED = 256
ROUTED_SCALE = 2.5
MOE_BLOCK = 64
ALPHA = (2.0 * DEPTH) ** 0.25
BETA = (8.0 * DEPTH) ** -0.25
LN_EPS = 1e-5
SPLIT_SIZES = (ATT_W, ATT_W, ATT_W, RET_QK_W, RET_QK_W, RET_V_W, RET_V_W, D_MODEL, D_MODEL)
D_IN = 3 * ATT_W + 2 * RET_QK_W + 2 * RET_V_W + 2 * D_MODEL

kernel_name = 'moba_retention_moe_hybrid_step'


def layer_norm(x):
    xf = x.astype(jnp.float32)
    mu = jnp.mean(xf, -1, keepdims=True)
    var = jnp.mean(jnp.square(xf - mu), -1, keepdims=True)
    return (xf - mu) * lax.rsqrt(var + LN_EPS)


def layer_norm_affine(x, g, b):
    return (layer_norm(x) * g + b).astype(x.dtype)


def ada_modulation(c, w_ada, b_ada):
    mod = jax.nn.silu(c) @ w_ada + b_ada
    return jnp.split(mod[:, None, :], 6, axis=-1)


def modulate(x, shift, scale):
    return (layer_norm(x) * (1.0 + scale) + shift).astype(x.dtype)


def t5_bucket(rel):
    n = jnp.maximum(rel, 0)
    max_exact = N_BUCKETS // 2
    ratio = jnp.log(jnp.maximum(n, max_exact).astype(jnp.float32) / max_exact) / math.log(MAX_DISTANCE / max_exact)
    large = jnp.minimum(max_exact + (ratio * (N_BUCKETS - max_exact)).astype(jnp.int32), N_BUCKETS - 1)
    return jnp.where(n < max_exact, n, large)


def select_blocks(q, means, n_past, ks):
    scores = jnp.einsum('nqhd,nbhd->nqhb', q.astype(jnp.float32), means)
    eligible = jnp.arange(means.shape[1]) < n_past
    scores = jnp.where(eligible, scores, -jnp.inf)
    _, idx = lax.top_k(scores, ks)
    return idx, idx < n_past


def moba_attend(q, q_pos, own_k, own_v, own_pos, rel_bias, sel):
    qf = q.astype(jnp.float32) * (HD_A ** -0.5)
    rel = q_pos[:, None] - own_pos[None, :]
    own_logit = jnp.einsum('nqhd,nkhd->nqhk', qf, own_k.astype(jnp.float32))
    own_logit = own_logit + rel_bias[t5_bucket(rel)].astype(jnp.float32).transpose(0, 2, 1)[None]
    own_logit = jnp.where((rel >= 0)[None, :, None, :], own_logit, -jnp.inf)
    if sel is None:
        p = jax.nn.softmax(own_logit, axis=-1)
        return jnp.einsum('nqhk,nkhd->nqhd', p, own_v.astype(jnp.float32)).astype(q.dtype)
    sel_k, sel_v, sel_pos, valid = sel
    h_i = jnp.arange(H_A)[None, None, :, None, None]
    sel_logit = jnp.einsum('nqhd,nqhskd->nqhsk', qf, sel_k.astype(jnp.float32))
    rel_s = q_pos[None, :, None, None, None] - sel_pos
    sel_logit = sel_logit + rel_bias[t5_bucket(rel_s), h_i].astype(jnp.float32)
    sel_logit = jnp.where(valid[..., None], sel_logit, -jnp.inf)
    n, nq, nh, ks, nb = sel_logit.shape
    logits = jnp.concatenate([sel_logit.reshape(n, nq, nh, ks * nb), own_logit], axis=-1)
    p = jax.nn.softmax(logits, axis=-1)
    p_sel = p[..., :ks * nb].reshape(n, nq, nh, ks, nb)
    p_own = p[..., ks * nb:]
    o = (jnp.einsum('nqhsk,nqhskd->nqhd', p_sel, sel_v.astype(jnp.float32))
         + jnp.einsum('nqhk,nkhd->nqhd', p_own, own_v.astype(jnp.float32)))
    return o.astype(q.dtype)


def moba_prompt(q, k, v, rel_bias):
    n, s = q.shape[:2]
    nb = -(-s // MOBA_BLOCK)
    pad = ((0, 0), (0, nb * MOBA_BLOCK - s), (0, 0), (0, 0))
    k_blk = jnp.pad(k, pad).reshape(n, nb, MOBA_BLOCK, H_A, HD_A)
    v_blk = jnp.pad(v, pad).reshape(n, nb, MOBA_BLOCK, H_A, HD_A)
    means = jnp.mean(k_blk.astype(jnp.float32), axis=2)
    ks = min(MOBA_TOPK, nb - 1)
    n_chunks = s // Q_BLOCK
    q_chunks = q.reshape(n, n_chunks, Q_BLOCK, H_A, HD_A).swapaxes(0, 1)
    n_i = jnp.arange(n)[:, None, None, None, None]
    h_i = jnp.arange(H_A)[None, None, :, None, None]
    offs = jnp.arange(MOBA_BLOCK)

    def one_chunk(args):
        qc, ci = args
        q_pos = ci * Q_BLOCK + jnp.arange(Q_BLOCK)
        kb = (ci * Q_BLOCK) // MOBA_BLOCK
        own_k = lax.dynamic_index_in_dim(k_blk, kb, axis=1, keepdims=False)
        own_v = lax.dynamic_index_in_dim(v_blk, kb, axis=1, keepdims=False)
        own_pos = kb * MOBA_BLOCK + offs
        sel = None
        if ks > 0:
            idx, valid = select_blocks(qc, means, kb, ks)
            blk = idx[..., None]
            rows = blk * MOBA_BLOCK + offs
            sel = (k_blk[n_i, blk, offs, h_i], v_blk[n_i, blk, offs, h_i], rows, valid)
        return moba_attend(qc, q_pos, own_k, own_v, own_pos, rel_bias, sel)

    out = lax.map(one_chunk, (q_chunks, jnp.arange(n_chunks)))
    return out.swapaxes(0, 1).reshape(n, s, ATT_W)


def moba_sample(q, k_new, v_new, cache_k, cache_v, page_table, layer, rel_bias):
    n, t = q.shape[:2]
    past_len = page_table.shape[1] * PAGE_SIZE
    b_own = past_len // MOBA_BLOCK
    own_start = b_own * MOBA_BLOCK
    q_pos = past_len + jnp.arange(t)
    p0, p1 = own_start // PAGE_SIZE, past_len // PAGE_SIZE
    n_own = (p1 - p0) * PAGE_SIZE
    own_pages = page_table[:, p0:p1]
    own_k = jnp.concatenate([cache_k[layer, own_pages].reshape(n, n_own, H_A, HD_A).astype(k_new.dtype), k_new], axis=1)
    own_v = jnp.concatenate([cache_v[layer, own_pages].reshape(n, n_own, H_A, HD_A).astype(v_new.dtype), v_new], axis=1)
    own_pos = jnp.concatenate([jnp.arange(own_start, past_len), q_pos])
    ks = min(MOBA_TOPK, b_own)
    sel = None
    if ks > 0:
        n_blk_pages = b_own * (MOBA_BLOCK // PAGE_SIZE)

        def seq_block_means(pt):
            rows = cache_k[layer, pt[:n_blk_pages]].astype(jnp.float32)
            return jnp.mean(rows.reshape(b_own, MOBA_BLOCK, H_A, HD_A), axis=1)

        means = lax.map(seq_block_means, page_table)
        idx, valid = select_blocks(q, means, b_own, ks)
        rows = idx[..., None] * MOBA_BLOCK + jnp.arange(MOBA_BLOCK)
        n_i = jnp.arange(n)[:, None, None, None, None]
        h_i = jnp.arange(H_A)[None, None, :, None, None]
        phys = page_table[n_i, rows // PAGE_SIZE]
        off = rows % PAGE_SIZE
        sel = (cache_k[layer, phys, off, h_i], cache_v[layer, phys, off, h_i], rows, valid)
    return moba_attend(q, q_pos, own_k, own_v, own_pos, rel_bias, sel).reshape(n, t, ATT_W)


def rotary(x, pos):
    theta = 1.0 / (ROPE_BASE ** jnp.linspace(0.0, 1.0, DK_R // 2, dtype=jnp.float32))
    ang = pos.astype(jnp.float32)[:, None] * theta[None, :]
    cos = jnp.cos(ang)[None, :, None, :]
    sin = jnp.sin(ang)[None, :, None, :]
    xf = x.astype(jnp.float32).reshape(x.shape[:-1] + (DK_R // 2, 2))
    x0, x1 = xf[..., 0], xf[..., 1]
    return jnp.stack([x0 * cos - x1 * sin, x0 * sin + x1 * cos], axis=-1).reshape(x.shape)


def retention_log_decay():
    return jnp.log1p(-jnp.exp2(-5.0 - jnp.arange(H_R, dtype=jnp.float32)))


def retention_chunk(q, k, v, state):
    log_g = retention_log_decay()
    c = q.shape[1]
    i = jnp.arange(c, dtype=jnp.float32)
    diff = i[:, None] - i[None, :]
    decay = jnp.where(diff >= 0, jnp.exp(jnp.maximum(diff, 0.0)[None] * log_g[:, None, None]), 0.0)
    vf = v.astype(jnp.float32)
    scores = jnp.einsum('nihd,njhd->nhij', q, k) * decay[None]
    inner = jnp.einsum('nhij,njhe->nihe', scores, vf)
    cross = jnp.einsum('nihd,nhde->nihe', q, state) * jnp.exp((i + 1.0)[:, None] * log_g[None, :])[None, :, :, None]
    k_w = jnp.exp((c - 1.0 - i)[:, None] * log_g[None, :])
    new_state = jnp.exp(c * log_g)[None, :, None, None] * state + jnp.einsum('njhd,njhe,jh->nhde', k, vf, k_w)
    return inner + cross, new_state


def retention_prompt(q, k, v):
    n, s = q.shape[:2]
    nc = s // RET_CHUNK

    def to_chunks(a):
        return a.reshape(n, nc, RET_CHUNK, H_R, a.shape[-1]).swapaxes(0, 1)

    def step(state, xs):
        qc, kc, vc = xs
        o, state = retention_chunk(qc, kc, vc, state)
        return state, o

    state0 = jnp.zeros((n, H_R, DK_R, DV_R), jnp.float32)
    state, o = lax.scan(step, state0, (to_chunks(q), to_chunks(k), to_chunks(v)))
    return o.swapaxes(0, 1).reshape(n, s, H_R, DV_R), state


def retention_readout(o, g, norm_g, norm_b):
    mu = jnp.mean(o, -1, keepdims=True)
    var = jnp.mean(jnp.square(o - mu), -1, keepdims=True)
    o = (o - mu) * lax.rsqrt(var + LN_EPS)
    n, s = o.shape[:2]
    o = o.reshape(n, s, RET_V_W) * norm_g + norm_b
    return jax.nn.silu(g.astype(jnp.float32)) * o


def route(h, w_router, router_bias):
    m = h.shape[0]
    s = jax.nn.sigmoid((h @ w_router).astype(jnp.float32))
    sb = s + router_bias.astype(jnp.float32)
    grp_score = jnp.sum(lax.top_k(sb.reshape(m, N_GROUPS, N_EXPERTS // N_GROUPS), 2)[0], axis=-1)
    _, top_g = lax.top_k(grp_score, TOPK_GROUPS)
    gmask = jnp.any(top_g[..., None] == jnp.arange(N_GROUPS), axis=-2)
    sb = jnp.where(jnp.repeat(gmask, N_EXPERTS // N_GROUPS, axis=-1), sb, -jnp.inf)
    _, idx = lax.top_k(sb, TOP_K)
    w = jnp.take_along_axis(s, idx, axis=-1)
    return idx, w / jnp.sum(w, -1, keepdims=True) * ROUTED_SCALE


def moe_ffn(h, w_router, router_bias, w_gate, w_up, w_down, ws_gate, ws_up, ws_down):
    m = h.shape[0]
    idx, gw = route(h, w_router, router_bias)
    n_pairs = m * TOP_K
    flat_e = idx.reshape(-1)
    flat_tok = jnp.repeat(jnp.arange(m, dtype=jnp.int32), TOP_K)
    flat_w = gw.reshape(-1)
    order = jnp.argsort(flat_e)
    se, stok, sw = flat_e[order], flat_tok[order], flat_w[order]
    counts = jnp.bincount(flat_e, length=N_EXPERTS)
    padded = (counts + MOE_BLOCK - 1) // MOE_BLOCK * MOE_BLOCK
    pad_end = jnp.cumsum(padded)
    pad_start = pad_end - padded
    start = jnp.cumsum(counts) - counts
    dest = pad_start[se] + jnp.arange(n_pairs) - start[se]
    n_blocks = -(-(n_pairs + N_EXPERTS * (MOE_BLOCK - 1)) // MOE_BLOCK)
    n_rows = n_blocks * MOE_BLOCK
    buf_tok = jnp.zeros((n_rows,), jnp.int32).at[dest].set(stok)
    buf_w = jnp.zeros((n_rows,), jnp.float32).at[dest].set(sw)
    blk_e = jnp.minimum(jnp.searchsorted(pad_end, jnp.arange(n_blocks) * MOE_BLOCK, side='right'), N_EXPERTS - 1)

    def expert_block(args):
        tok, e, w = args
        xb = h[tok]
        a = jax.nn.silu(xb @ w_gate[e]) * (xb @ w_up[e])
        return (a @ w_down[e]).astype(jnp.float32) * w[:, None]

    ys = lax.map(expert_block, (buf_tok.reshape(n_blocks, MOE_BLOCK), blk_e, buf_w.reshape(n_blocks, MOE_BLOCK)))
    routed = jnp.zeros((m, D_MODEL), jnp.float32).at[buf_tok].add(ys.reshape(n_rows, D_MODEL))
    shared = (jax.nn.silu(h @ ws_gate) * (h @ ws_up)) @ ws_down
    return (routed + shared.astype(jnp.float32)).astype(h.dtype)


def mixer_inputs(x, c, w_ada, b_ada, w_in):
    mods = ada_modulation(c, w_ada, b_ada)
    h = modulate(x, mods[0], mods[1])
    n, s = x.shape[:2]
    points = [int(p) for p in np.cumsum(SPLIT_SIZES)[:-1]]
    qa, ka, va, qr, kr, vr, gr, ga, gb = jnp.split(h @ w_in, points, axis=-1)
    heads = (qa.reshape(n, s, H_A, HD_A), ka.reshape(n, s, H_A, HD_A), va.reshape(n, s, H_A, HD_A),
             qr.reshape(n, s, H_R, DK_R), kr.reshape(n, s, H_R, DK_R), vr.reshape(n, s, H_R, DV_R))
    return heads, (gr, ga, gb), mods


def layer_output(x, mods, attn_o, ret_o, gates, lw):
    gr, ga, gb = gates
    gate1, shift2, scale2, gate2 = mods[2], mods[3], mods[4], mods[5]
    n, s = x.shape[:2]
    ya = attn_o @ lw['w_branch_a']
    yb = retention_readout(ret_o, gr, lw['ret_norm_g'], lw['ret_norm_b']).astype(x.dtype) @ lw['w_branch_b']
    mixed = (jax.nn.sigmoid(ga) * ya + jax.nn.sigmoid(gb) * yb) @ lw['w_out']
    x1 = layer_norm_affine(ALPHA * x + gate1 * mixed, lw['ln1_g'], lw['ln1_b'])
    h2 = modulate(x1, shift2, scale2)
    f = moe_ffn(h2.reshape(n * s, D_MODEL), lw['w_router'], lw['router_bias'], lw['w_gate'], lw['w_up'],
                lw['w_down'], lw['w_shared_gate'], lw['w_shared_up'], lw['w_shared_down']).reshape(n, s, D_MODEL)
    return layer_norm_affine(ALPHA * x1 + gate2 * f, lw['ln2_g'], lw['ln2_b'])


def setup_inputs(seed: int = 0) -> dict:
    key = jax.random.key(seed)
    ks = jax.random.split(key, 32)
    f32 = jnp.float32
    n_pages = PAST_LEN // PAGE_SIZE
    n_used = DEC_BATCH * n_pages
    n_pool = n_used + (n_used + 3) // 4

    def nrm(k, shape, scale):
        return jax.random.normal(k, shape, f32) * scale

    page_table = jax.random.permutation(ks[0], n_pool)[:n_used].reshape(DEC_BATCH, n_pages).astype(jnp.int32)
    return {
        'x_prompt': nrm(ks[1], (BATCH, SEQ, D_MODEL), 1.0),
        'x_sample': nrm(ks[2], (DEC_BATCH, DEC_SEQ, D_MODEL), 1.0),
        'cache_k': nrm(ks[3], (DEPTH, n_pool, PAGE_SIZE, H_A, HD_A), 1.0),
        'cache_v': nrm(ks[4], (DEPTH, n_pool, PAGE_SIZE, H_A, HD_A), 1.0),
        'state_ret': nrm(ks[5], (DEPTH, DEC_BATCH, H_R, DK_R, DV_R), 0.1),
        'page_table': page_table,
        'c_prompt': nrm(ks[6], (BATCH, D_MODEL), 1.0),
        'c_sample': nrm(ks[7], (DEC_BATCH, D_MODEL), 1.0),
        'w_ada': nrm(ks[8], (DEPTH, D_MODEL, 6 * D_MODEL), 0.5 * D_MODEL ** -0.5),
        'b_ada': nrm(ks[9], (DEPTH, 6 * D_MODEL), 0.02),
        'w_in': nrm(ks[10], (DEPTH, D_MODEL, D_IN), D_MODEL ** -0.5),
        'rel_bias': nrm(ks[11], (N_BUCKETS, H_A), 0.5),
        'ret_norm_g': 1.0 + nrm(ks[12], (DEPTH, RET_V_W), 0.02),
        'ret_norm_b': nrm(ks[13], (DEPTH, RET_V_W), 0.02),
        'w_branch_a': nrm(ks[14], (DEPTH, ATT_W, D_MODEL), ATT_W ** -0.5),
        'w_branch_b': nrm(ks[15], (DEPTH, RET_V_W, D_MODEL), RET_V_W ** -0.5),
        'w_out': nrm(ks[16], (DEPTH, D_MODEL, D_MODEL), BETA * D_MODEL ** -0.5),
        'ln1_g': 1.0 + nrm(ks[17], (DEPTH, D_MODEL), 0.02),
        'ln1_b': nrm(ks[18], (DEPTH, D_MODEL), 0.02),
        'w_router': nrm(ks[19], (DEPTH, D_MODEL, N_EXPERTS), D_MODEL ** -0.5),
        'router_bias': nrm(ks[20], (DEPTH, N_EXPERTS), 0.01),
        'w_gate': nrm(ks[21], (DEPTH, N_EXPERTS, D_MODEL, D_EXPERT), D_MODEL ** -0.5),
        'w_up': nrm(ks[22], (DEPTH, N_EXPERTS, D_MODEL, D_EXPERT), D_MODEL ** -0.5),
        'w_down': nrm(ks[23], (DEPTH, N_EXPERTS, D_EXPERT, D_MODEL), BETA * D_EXPERT ** -0.5),
        'w_shared_gate': nrm(ks[24], (DEPTH, D_MODEL, D_SHARED), D_MODEL ** -0.5),
        'w_shared_up': nrm(ks[25], (DEPTH, D_MODEL, D_SHARED), D_MODEL ** -0.5),
        'w_shared_down': nrm(ks[26], (DEPTH, D_SHARED, D_MODEL), BETA * D_SHARED ** -0.5),
        'ln2_g': 1.0 + nrm(ks[27], (DEPTH, D_MODEL), 0.02),
        'ln2_b': nrm(ks[28], (DEPTH, D_MODEL), 0.02),
    }


def reference(x_prompt, x_sample, cache_k, cache_v, state_ret, page_table, c_prompt, c_sample,
              w_ada, b_ada, w_in, rel_bias, ret_norm_g, ret_norm_b, w_branch_a, w_branch_b, w_out,
              ln1_g, ln1_b, w_router, router_bias, w_gate, w_up, w_down,
              w_shared_gate, w_shared_up, w_shared_down, ln2_g, ln2_b):
    past_len = page_table.shape[1] * PAGE_SIZE
    pos_p = jnp.arange(x_prompt.shape[1])
    pos_s = past_len + jnp.arange(x_sample.shape[1])
    xp, xs = x_prompt, x_sample
    kp_l, vp_l, sp_l, ks_l, vs_l, ss_l = [], [], [], [], [], []
    for l in range(DEPTH):
        lw = {'ret_norm_g': ret_norm_g[l], 'ret_norm_b': ret_norm_b[l], 'w_branch_a': w_branch_a[l],
              'w_branch_b': w_branch_b[l], 'w_out': w_out[l], 'ln1_g': ln1_g[l], 'ln1_b': ln1_b[l],
              'w_router': w_router[l], 'router_bias': router_bias[l], 'w_gate': w_gate[l], 'w_up': w_up[l],
              'w_down': w_down[l], 'w_shared_gate': w_shared_gate[l], 'w_shared_up': w_shared_up[l],
              'w_shared_down': w_shared_down[l], 'ln2_g': ln2_g[l], 'ln2_b': ln2_b[l]}
        (qa, ka, va, qr, kr, vr), gates, mods = mixer_inputs(xp, c_prompt, w_ada[l], b_ada[l], w_in[l])
        attn_o = moba_prompt(qa, ka, va, rel_bias)
        ret_o, st_p = retention_prompt(rotary(qr, pos_p), rotary(kr, pos_p) * DK_R ** -0.5, vr)
        xp = layer_output(xp, mods, attn_o, ret_o, gates, lw)
        kp_l.append(ka)
        vp_l.append(va)
        sp_l.append(st_p)
        (qa, ka, va, qr, kr, vr), gates, mods = mixer_inputs(xs, c_sample, w_ada[l], b_ada[l], w_in[l])
        attn_o = moba_sample(qa, ka, va, cache_k, cache_v, page_table, l, rel_bias)
        ret_o, st_s = retention_chunk(rotary(qr, pos_s), rotary(kr, pos_s) * DK_R ** -0.5, vr,
                                      state_ret[l].astype(jnp.float32))
        xs = layer_output(xs, mods, attn_o, ret_o, gates, lw)
        ks_l.append(ka)
        vs_l.append(va)
        ss_l.append(st_s)
    k_prompt = jnp.stack(kp_l)
    v_prompt = jnp.stack(vp_l)
    ret_prompt = jnp.stack(sp_l)
    k_sample = jnp.stack(ks_l)
    v_sample = jnp.stack(vs_l)
    ret_sample = jnp.stack(ss_l)
    return (xp, xs, k_prompt, v_prompt, ret_prompt, k_sample, v_sample, ret_sample)
```

```python
import functools
import math

import jax
import jax.numpy as jnp
import numpy as np
from jax import lax
from jax.experimental import pallas as pl
from jax.experimental.pallas import tpu as pltpu

D_MODEL = 1024
DEPTH = 1
PAGE_SIZE = 128
H_A = 8
HD_A = 64
ATT_W = H_A * HD_A
MOBA_BLOCK = 256
MOBA_TOPK = 3
N_BUCKETS = 32
MAX_DISTANCE = 128
H_R = 4
DK_R = 128
DV_R = 128
RET_QK_W = H_R * DK_R
RET_V_W = H_R * DV_R
RET_CHUNK = 128
ROPE_BASE = 10000.0
N_EXPERTS = 256
TOP_K = 8
N_GROUPS = 8
TOPK_GROUPS = 4
GROUP_SIZE = N_EXPERTS // N_GROUPS
D_EXPERT = 256
D_SHARED = 256
ROUTED_SCALE = 2.5
ALPHA = (2.0 * DEPTH) ** 0.25
LN_EPS = 1e-5
D_IN = 3 * ATT_W + 2 * RET_QK_W + 2 * RET_V_W + 2 * D_MODEL

F32 = jnp.float32
BF16 = jnp.bfloat16
NEG_INF = float("-inf")
HIGHEST = lax.Precision.HIGHEST
NT_DIMS = (((1,), (1,)), ((), ()))
TN_DIMS = (((0,), (0,)), ((), ()))

VMEM_LIMIT_BYTES = 56 * 1024 * 1024
ROW_TILE = 512
MOE_ROW_TILE = 256


def _params(*semantics):
    return pltpu.CompilerParams(dimension_semantics=semantics, vmem_limit_bytes=VMEM_LIMIT_BYTES)


def _layer_norm(x):
    mu = jnp.mean(x, axis=-1, keepdims=True)
    xc = x - mu
    var = jnp.mean(xc * xc, axis=-1, keepdims=True)
    return xc * lax.rsqrt(var + LN_EPS)


def _silu(x):
    return x * jax.nn.sigmoid(x)


def _resident(shape):
    zeros = (0,) * len(shape)
    return pl.BlockSpec(shape, lambda *_: zeros, pipeline_mode=pl.Buffered(1))


def _ada_kernel(c_ref, w_ref, b_ref, o_ref):
    a = _silu(c_ref[...])
    o_ref[...] = jnp.dot(a, w_ref[...], precision=HIGHEST, preferred_element_type=F32) + b_ref[...]


def ada_modulation(c, w_ada, b_ada):
    n = c.shape[0]
    tn = 1536
    return pl.pallas_call(
        _ada_kernel,
        grid=(6 * D_MODEL // tn,),
        in_specs=[pl.BlockSpec((n, D_MODEL), lambda j: (0, 0)),
                  pl.BlockSpec((D_MODEL, tn), lambda j: (0, j)),
                  pl.BlockSpec((1, tn), lambda j: (0, j))],
        out_specs=pl.BlockSpec((n, tn), lambda j: (0, j)),
        out_shape=jax.ShapeDtypeStruct((n, 6 * D_MODEL), F32),
        compiler_params=_params("arbitrary"),
        name="ada_modulation",
    )(c, w_ada, b_ada.reshape(1, -1))


def _in_proj_kernel(x_ref, shift_ref, scale_ref, w_ref,
                    qa_ref, ka_ref, va_ref, kbf_ref, vbf_ref, kmean_ref,
                    qr_ref, kr_ref, vr_ref, gr_ref, ga_ref, gb_ref):
    h = (_layer_norm(x_ref[...]) * (1.0 + scale_ref[0]) + shift_ref[0]).astype(BF16)

    def proj(start, width):
        return jnp.dot(h, w_ref[:, start:start + width], preferred_element_type=F32)

    qa_ref[...] = proj(0, ATT_W)
    ka = proj(ATT_W, ATT_W)
    ka_ref[...] = ka
    kbf_ref[...] = ka.astype(BF16)
    n_blk = ka.shape[0] // MOBA_BLOCK
    for b in range(n_blk):
        kmean_ref[b] = jnp.mean(ka[b * MOBA_BLOCK:(b + 1) * MOBA_BLOCK], axis=0, keepdims=True)
    va = proj(2 * ATT_W, ATT_W)
    va_ref[...] = va
    vbf_ref[...] = va.astype(BF16)
    off = 3 * ATT_W
    qr_ref[...] = proj(off, RET_QK_W)
    kr_ref[...] = proj(off + RET_QK_W, RET_QK_W)
    off += 2 * RET_QK_W
    vr_ref[...] = proj(off, RET_V_W).astype(BF16)
    gr_ref[...] = proj(off + RET_V_W, RET_V_W)
    off += 2 * RET_V_W
    ga_ref[...] = proj(off, D_MODEL)
    gb_ref[...] = proj(off + D_MODEL, D_MODEL)


def in_proj(x, shift, scale, w_in_bf16, tiles_per_group):
    m = x.shape[0]
    tm = ROW_TILE
    r = shift.shape[1]
    row = lambda w: pl.BlockSpec((tm, w), lambda i: (i, 0))
    mod = pl.BlockSpec((1, r, D_MODEL), lambda i: (i // tiles_per_group, 0, 0))
    widths = dict(qa=ATT_W, ka=ATT_W, va=ATT_W, kbf=ATT_W, vbf=ATT_W, qr=RET_QK_W, kr=RET_QK_W,
                  vr=RET_V_W, gr=RET_V_W, ga=D_MODEL, gb=D_MODEL)
    dtypes = dict(kbf=BF16, vbf=BF16, vr=BF16)
    names = ["qa", "ka", "va", "kbf", "vbf", "kmean", "qr", "kr", "vr", "gr", "ga", "gb"]
    out_specs, out_shapes = [], []
    for nm in names:
        if nm == "kmean":
            out_specs.append(pl.BlockSpec((tm // MOBA_BLOCK, 1, ATT_W), lambda i: (i, 0, 0)))
            out_shapes.append(jax.ShapeDtypeStruct((m // MOBA_BLOCK, 1, ATT_W), F32))
        else:
            out_specs.append(row(widths[nm]))
            out_shapes.append(jax.ShapeDtypeStruct((m, widths[nm]), dtypes.get(nm, F32)))
    outs = pl.pallas_call(
        _in_proj_kernel,
        grid=(m // tm,),
        in_specs=[row(D_MODEL), mod, mod, _resident((D_MODEL, D_IN))],
        out_specs=out_specs,
        out_shape=out_shapes,
        compiler_params=_params("arbitrary"),
        name="in_proj",
    )(x, shift, scale, w_in_bf16)
    return dict(zip(names, outs))


def t5_bucket(rel):
    n = jnp.maximum(rel, 0)
    max_exact = N_BUCKETS // 2
    ratio = jnp.log(jnp.maximum(n, max_exact).astype(F32) / max_exact) / math.log(MAX_DISTANCE / max_exact)
    large = jnp.minimum(max_exact + (ratio * (N_BUCKETS - max_exact)).astype(jnp.int32), N_BUCKETS - 1)
    return jnp.where(n < max_exact, n, large)


def _bias_from_buckets(bucket, rel_bias_ref, head):
    out = jnp.zeros(bucket.shape, F32)
    for b in range(N_BUCKETS):
        out = jnp.where(bucket == b, rel_bias_ref[b, head], out)
    return out


def _top_blocks(scores, n_eligible, n_pick):
    n_blocks = scores.shape[1]
    blk = lax.broadcasted_iota(jnp.int32, scores.shape, 1)
    sc = jnp.where(blk < n_eligible, scores, NEG_INF)
    sel = jnp.zeros(scores.shape, jnp.bool_)
    for _ in range(n_pick):
        best = jnp.max(sc, axis=1, keepdims=True)
        first = jnp.min(jnp.where(sc == best, blk, n_blocks), axis=1, keepdims=True)
        hit = blk == first
        sel = jnp.logical_or(sel, jnp.logical_and(hit, first < n_eligible))
        sc = jnp.where(hit, NEG_INF, sc)
    return sel


def _moba_prompt_kernel(relb_ref, q_ref, k_ref, v_ref, kmean_ref, bkt_own_ref, bkt_prev_ref,
                        o_ref, bias_own, bias_prev):
    n_i = pl.program_id(0)
    kb = pl.program_id(1)
    n_blocks = kmean_ref.shape[1]
    blk_rows = MOBA_BLOCK

    @pl.when(jnp.logical_and(n_i == 0, kb == 0))
    def _():
        for h in range(H_A):
            bias_own[h] = _bias_from_buckets(bkt_own_ref[...], relb_ref, h)
            bias_prev[h] = _bias_from_buckets(bkt_prev_ref[...], relb_ref, h)

    q = q_ref[0]
    kmean = kmean_ref[0]
    row = lax.broadcasted_iota(jnp.int32, (blk_rows, blk_rows), 0)
    col = lax.broadcasted_iota(jnp.int32, (blk_rows, blk_rows), 1)
    causal = row >= col
    blk_ids = lax.broadcasted_iota(jnp.int32, (blk_rows, n_blocks), 1)
    own_start = pl.multiple_of(kb * blk_rows, blk_rows)
    prev_blk = jnp.maximum(kb - 1, 0)
    prev_start = pl.multiple_of(prev_blk * blk_rows, blk_rows)
    outs = []
    for h in range(H_A):
        lanes = slice(h * HD_A, (h + 1) * HD_A)
        qh = q[:, lanes]
        sel = _top_blocks(lax.dot_general(qh, kmean[:, lanes], NT_DIMS, precision=HIGHEST,
                                          preferred_element_type=F32), kb, MOBA_TOPK)
        sel_f = sel.astype(F32)
        qs = (qh * (HD_A ** -0.5)).astype(BF16)

        def attend(k_start, bias, keep, carry):
            m, l, acc = carry
            kh = k_ref[0, pl.ds(k_start, blk_rows), lanes]
            vh = v_ref[0, pl.ds(k_start, blk_rows), lanes]
            s = lax.dot_general(qs, kh, NT_DIMS, preferred_element_type=F32) + bias
            s = jnp.where(keep, s, NEG_INF)
            m_new = jnp.maximum(m, jnp.max(s, axis=1, keepdims=True))
            alpha = jnp.exp(m - m_new)
            p = jnp.exp(s - m_new)
            l = alpha * l + jnp.sum(p, axis=1, keepdims=True)
            acc = alpha * acc + jnp.dot(p.astype(BF16), vh, preferred_element_type=F32)
            return m_new, l, acc

        s0 = lax.dot_general(qs, k_ref[0, pl.ds(own_start, blk_rows), lanes], NT_DIMS,
                             preferred_element_type=F32) + bias_own[h]
        s0 = jnp.where(causal, s0, NEG_INF)
        m0 = jnp.max(s0, axis=1, keepdims=True)
        p0 = jnp.exp(s0 - m0)
        carry = (m0, jnp.sum(p0, axis=1, keepdims=True),
                 jnp.dot(p0.astype(BF16), v_ref[0, pl.ds(own_start, blk_rows), lanes],
                         preferred_element_type=F32))

        def column(j):
            return jnp.max(jnp.where(blk_ids == j, sel_f, 0.0), axis=1, keepdims=True) > 0.5

        keep_prev = jnp.logical_and(column(prev_blk), kb >= 1)
        carry = attend(prev_start, bias_prev[h], keep_prev, carry)
        far_bias = relb_ref[N_BUCKETS - 1, h]

        def far_body(j, c):
            return attend(pl.multiple_of(j * blk_rows, blk_rows), far_bias, column(j), c)

        m, l, acc = lax.fori_loop(0, jnp.maximum(kb - 1, 0), far_body, carry)
        outs.append(acc / l)
    o_ref[0] = jnp.concatenate(outs, axis=-1).astype(o_ref.dtype)


def moba_prompt(q, kbf, vbf, kmean, rel_bias):
    n, s, _ = q.shape
    nb = s // MOBA_BLOCK
    i = jnp.arange(MOBA_BLOCK)
    rel_own = i[:, None] - i[None, :]
    bkt_own = t5_bucket(rel_own).astype(jnp.int32)
    bkt_prev = t5_bucket(rel_own + MOBA_BLOCK).astype(jnp.int32)
    grid_spec = pltpu.PrefetchScalarGridSpec(
        num_scalar_prefetch=1,
        grid=(n, nb),
        in_specs=[pl.BlockSpec((1, MOBA_BLOCK, ATT_W), lambda a, b, *_: (a, b, 0)),
                  pl.BlockSpec((1, s, ATT_W), lambda a, b, *_: (a, 0, 0)),
                  pl.BlockSpec((1, s, ATT_W), lambda a, b, *_: (a, 0, 0)),
                  pl.BlockSpec((1, nb, ATT_W), lambda a, b, *_: (a, 0, 0)),
                  pl.BlockSpec((MOBA_BLOCK, MOBA_BLOCK), lambda a, b, *_: (0, 0)),
                  pl.BlockSpec((MOBA_BLOCK, MOBA_BLOCK), lambda a, b, *_: (0, 0))],
        out_specs=pl.BlockSpec((1, MOBA_BLOCK, ATT_W), lambda a, b, *_: (a, b, 0)),
        scratch_shapes=[pltpu.VMEM((H_A, MOBA_BLOCK, MOBA_BLOCK), F32),
                        pltpu.VMEM((H_A, MOBA_BLOCK, MOBA_BLOCK), F32)],
    )
    return pl.pallas_call(
        _moba_prompt_kernel,
        grid_spec=grid_spec,
        out_shape=jax.ShapeDtypeStruct((n, s, ATT_W), BF16),
        compiler_params=_params("arbitrary", "arbitrary"),
        name="moba_prompt",
    )(rel_bias, q, kbf, vbf, kmean, bkt_own, bkt_prev)


def _swap_pairs(x):
    lane = lax.broadcasted_iota(jnp.int32, x.shape, 1)
    width = x.shape[1]
    return jnp.where(lane % 2 == 0, pltpu.roll(x, width - 1, 1), pltpu.roll(x, 1, 1))


def _retention_kernel(q_ref, k_ref, v_ref, g_ref, cos_ref, sin_ref, decay_ref, cross_ref, kw_ref,
                      sdec_ref, ng_ref, nb_ref, st_in_ref, y_ref, st_out_ref, st):
    c = pl.program_id(1)

    @pl.when(c == 0)
    def _():
        st[...] = st_in_ref[0]

    cos = cos_ref[...]
    sin = sin_ref[...]
    q = q_ref[0]
    k = k_ref[0]
    v = v_ref[0]
    g = g_ref[0]
    for h in range(H_R):
        lanes = slice(h * DK_R, (h + 1) * DK_R)
        qh = q[:, lanes]
        kh = k[:, lanes]
        qh = (qh * cos + _swap_pairs(qh) * sin).astype(BF16)
        kh = (kh * cos + _swap_pairs(kh) * sin) * (DK_R ** -0.5)
        vh = v[:, lanes]
        state = st[h]
        scores = lax.dot_general(qh, kh.astype(BF16), NT_DIMS, preferred_element_type=F32) * decay_ref[h]
        inner = jnp.dot(scores.astype(BF16), vh, preferred_element_type=F32)
        cross = jnp.dot(qh, state.astype(BF16), preferred_element_type=F32) * cross_ref[:, lanes]
        kd = (kh * kw_ref[:, lanes]).astype(BF16)
        st[h] = sdec_ref[h] * state + lax.dot_general(kd, vh, TN_DIMS, preferred_element_type=F32)
        o = inner + cross
        o = _layer_norm(o) * ng_ref[:, lanes] + nb_ref[:, lanes]
        y_ref[0, :, lanes] = (_silu(g[:, lanes]) * o).astype(y_ref.dtype)

    @pl.when(c == pl.num_programs(1) - 1)
    def _():
        st_out_ref[0] = st[...]


def _retention_tables(pos, n_valid, chunk):
    theta = 1.0 / (ROPE_BASE ** jnp.linspace(0.0, 1.0, DK_R // 2, dtype=F32))
    ang = pos.astype(F32)[:, None] * theta[None, :]
    cos = jnp.repeat(jnp.cos(ang), 2, axis=1)
    sin = jnp.repeat(jnp.sin(ang), 2, axis=1) * jnp.tile(jnp.array([-1.0, 1.0], F32), DK_R // 2)[None, :]
    log_g = jnp.log1p(-jnp.exp2(-5.0 - jnp.arange(H_R, dtype=F32)))
    i = jnp.arange(chunk, dtype=F32)
    valid = i < n_valid
    diff = i[:, None] - i[None, :]
    decay = jnp.where(diff >= 0, jnp.exp(jnp.maximum(diff, 0.0)[None] * log_g[:, None, None]), 0.0)
    decay = jnp.where(valid[None, :, None] & valid[None, None, :], decay, 0.0)
    cross = jnp.where(valid[:, None], jnp.exp((i + 1.0)[:, None] * log_g[None, :]), 0.0)
    kw = jnp.where(valid[:, None], jnp.exp((n_valid - 1.0 - i)[:, None] * log_g[None, :]), 0.0)
    sdec = jnp.exp(n_valid * log_g)
    return dict(cos=cos, sin=sin, decay=decay,
                cross=jnp.repeat(cross, DV_R, axis=1), kw=jnp.repeat(kw, DK_R, axis=1),
                sdec=jnp.broadcast_to(sdec[:, None, None], (H_R, 1, DV_R)))


def retention(q, k, v, g, state0, tables, norm_g, norm_b):
    n, s, _ = q.shape
    chunk = tables["decay"].shape[1]
    n_chunks = s // chunk
    seq = pl.BlockSpec((1, chunk, RET_QK_W), lambda a, c: (a, c, 0))
    per_chunk = pl.BlockSpec((chunk, DK_R), lambda a, c: (c, 0))
    per_row = pl.BlockSpec((chunk, RET_QK_W), lambda a, c: (0, 0))
    st_spec = pl.BlockSpec((1, H_R, DK_R, DV_R), lambda a, c: (a, 0, 0, 0))
    vec = pl.BlockSpec((1, RET_V_W), lambda a, c: (0, 0))
    return pl.pallas_call(
        _retention_kernel,
        grid=(n, n_chunks),
        in_specs=[seq, seq, seq, seq, per_chunk, per_chunk,
                  pl.BlockSpec((H_R, chunk, chunk), lambda a, c: (0, 0, 0)),
                  per_row, per_row,
                  pl.BlockSpec((H_R, 1, DV_R), lambda a, c: (0, 0, 0)),
                  vec, vec, st_spec],
        out_specs=[seq, st_spec],
        out_shape=[jax.ShapeDtypeStruct((n, s, RET_V_W), BF16),
                   jax.ShapeDtypeStruct((n, H_R, DK_R, DV_R), F32)],
        scratch_shapes=[pltpu.VMEM((H_R, DK_R, DV_R), F32)],
        compiler_params=_params("arbitrary", "arbitrary"),
        name="retention",
    )(q, k, v, g, tables["cos"], tables["sin"], tables["decay"], tables["cross"], tables["kw"],
      tables["sdec"], norm_g.reshape(1, -1), norm_b.reshape(1, -1), state0)


def _mix_kernel(att_ref, ret_ref, ga_ref, gb_ref, x_ref, gate1_ref, shift2_ref, scale2_ref,
                wa_ref, wb_ref, wo_ref, g1_ref, b1_ref, wr_ref, x1_ref, h2_ref, logit_ref):
    ya = jnp.dot(att_ref[...], wa_ref[...], preferred_element_type=F32)
    yb = jnp.dot(ret_ref[...], wb_ref[...], preferred_element_type=F32)
    mixed_in = jax.nn.sigmoid(ga_ref[...]) * ya + jax.nn.sigmoid(gb_ref[...]) * yb
    mixed = jnp.dot(mixed_in.astype(BF16), wo_ref[...], preferred_element_type=F32)
    x1 = _layer_norm(ALPHA * x_ref[...] + gate1_ref[0] * mixed) * g1_ref[...] + b1_ref[...]
    x1_ref[...] = x1
    h2 = _layer_norm(x1) * (1.0 + scale2_ref[0]) + shift2_ref[0]
    h2_ref[...] = h2.astype(BF16)
    logit_ref[...] = jnp.dot(h2, wr_ref[...], precision=HIGHEST, preferred_element_type=F32)


def mix_out(att, ret, ga, gb, x, gate1, shift2, scale2, wa, wb, wo, ln_g, ln_b, w_router, tiles_per_group):
    m = x.shape[0]
    tm = ROW_TILE
    r = gate1.shape[1]
    row = lambda w: pl.BlockSpec((tm, w), lambda i: (i, 0))
    mod = pl.BlockSpec((1, r, D_MODEL), lambda i: (i // tiles_per_group, 0, 0))
    return pl.pallas_call(
        _mix_kernel,
        grid=(m // tm,),
        in_specs=[row(ATT_W), row(RET_V_W), row(D_MODEL), row(D_MODEL), row(D_MODEL), mod, mod, mod,
                  _resident((ATT_W, D_MODEL)), _resident((RET_V_W, D_MODEL)), _resident((D_MODEL, D_MODEL)),
                  _resident((1, D_MODEL)), _resident((1, D_MODEL)), _resident((D_MODEL, N_EXPERTS))],
        out_specs=[row(D_MODEL), row(D_MODEL), row(N_EXPERTS)],
        out_shape=[jax.ShapeDtypeStruct((m, D_MODEL), F32), jax.ShapeDtypeStruct((m, D_MODEL), BF16),
                   jax.ShapeDtypeStruct((m, N_EXPERTS), F32)],
        compiler_params=_params("arbitrary"),
        name="mix_out",
    )(att, ret, ga, gb, x, gate1, shift2, scale2, wa, wb, wo, ln_g.reshape(1, -1), ln_b.reshape(1, -1),
      w_router)


def _first_max(x, ids, sentinel):
    best = jnp.max(x, axis=0, keepdims=True)
    first = jnp.min(jnp.where(x == best, ids, sentinel), axis=0, keepdims=True)
    return best, first


def _route_kernel(logit_ref, bias_ref, idx_ref, gate_ref):
    s = jax.nn.sigmoid(logit_ref[...].T)
    sb = s + bias_ref[...]
    tm = s.shape[1]
    gid = lax.broadcasted_iota(jnp.int32, (GROUP_SIZE, tm), 0)
    grp_scores = []
    for gi in range(N_GROUPS):
        grp = sb[gi * GROUP_SIZE:(gi + 1) * GROUP_SIZE]
        top1, first = _first_max(grp, gid, GROUP_SIZE)
        top2 = jnp.max(jnp.where(gid == first, NEG_INF, grp), axis=0, keepdims=True)
        grp_scores.append(top1 + top2)
    gs = jnp.concatenate(grp_scores, axis=0)
    gids = lax.broadcasted_iota(jnp.int32, (N_GROUPS, tm), 0)
    gmask = jnp.zeros((N_GROUPS, tm), jnp.bool_)
    for _ in range(TOPK_GROUPS):
        _, first = _first_max(gs, gids, N_GROUPS)
        hit = gids == first
        gmask = jnp.logical_or(gmask, hit)
        gs = jnp.where(hit, NEG_INF, gs)
    masked = jnp.concatenate(
        [jnp.where(gmask[gi:gi + 1], sb[gi * GROUP_SIZE:(gi + 1) * GROUP_SIZE], NEG_INF)
         for gi in range(N_GROUPS)], axis=0)
    eids = lax.broadcasted_iota(jnp.int32, (N_EXPERTS, tm), 0)
    picks, weights = [], []
    for _ in range(TOP_K):
        _, first = _first_max(masked, eids, N_EXPERTS)
        hit = eids == first
        picks.append(first)
        weights.append(jnp.sum(jnp.where(hit, s, 0.0), axis=0, keepdims=True))
        masked = jnp.where(hit, NEG_INF, masked)
    w = jnp.concatenate(weights, axis=0)
    idx_ref[...] = jnp.concatenate(picks, axis=0)
    gate_ref[...] = w / jnp.sum(w, axis=0, keepdims=True) * ROUTED_SCALE


def route(logits, router_bias):
    m = logits.shape[0]
    tm = ROW_TILE
    return pl.pallas_call(
        _route_kernel,
        grid=(m // tm,),
        in_specs=[pl.BlockSpec((tm, N_EXPERTS), lambda i: (i, 0)),
                  pl.BlockSpec((N_EXPERTS, 1), lambda i: (0, 0))],
        out_specs=[pl.BlockSpec((TOP_K, tm), lambda i: (0, i)), pl.BlockSpec((TOP_K, tm), lambda i: (0, i))],
        out_shape=[jax.ShapeDtypeStruct((TOP_K, m), jnp.int32), jax.ShapeDtypeStruct((TOP_K, m), F32)],
        compiler_params=_params("arbitrary"),
        name="route",
    )(logits, router_bias.reshape(-1, 1))


def _expert_kernel(blk_e_ref, n_used_ref, x_ref, wrow_ref, wg_ref, wu_ref, wd_ref, y_ref):
    i = pl.program_id(0)

    @pl.when(i < n_used_ref[0])
    def _():
        x = x_ref[...]
        a = (_silu(jnp.dot(x, wg_ref[0].astype(BF16), preferred_element_type=F32))
             * jnp.dot(x, wu_ref[0].astype(BF16), preferred_element_type=F32))
        y = jnp.dot(a.astype(BF16), wd_ref[0].astype(BF16), preferred_element_type=F32)
        y_ref[...] = y * wrow_ref[...]

    @pl.when(i >= n_used_ref[0])
    def _():
        y_ref[...] = jnp.zeros_like(y_ref)


def routed_experts(xs, wrow, blk_e, n_used, w_gate, w_up, w_down):
    n_rows = xs.shape[0]
    tm = MOE_ROW_TILE
    grid_spec = pltpu.PrefetchScalarGridSpec(
        num_scalar_prefetch=2,
        grid=(n_rows // tm,),
        in_specs=[pl.BlockSpec((tm, D_MODEL), lambda i, be, nu: (i, 0)),
                  pl.BlockSpec((tm, 1), lambda i, be, nu: (i, 0)),
                  pl.BlockSpec((1, D_MODEL, D_EXPERT), lambda i, be, nu: (be[i], 0, 0)),
                  pl.BlockSpec((1, D_MODEL, D_EXPERT), lambda i, be, nu: (be[i], 0, 0)),
                  pl.BlockSpec((1, D_EXPERT, D_MODEL), lambda i, be, nu: (be[i], 0, 0))],
        out_specs=pl.BlockSpec((tm, D_MODEL), lambda i, be, nu: (i, 0)),
    )
    return pl.pallas_call(
        _expert_kernel,
        grid_spec=grid_spec,
        out_shape=jax.ShapeDtypeStruct((n_rows, D_MODEL), F32),
        compiler_params=_params("arbitrary"),
        name="routed_experts",
    )(blk_e, n_used, xs, wrow, w_gate, w_up, w_down)


def _final_kernel(x1_ref, routed_ref, h2_ref, gate2_ref, wsg_ref, wsu_ref, wsd_ref, g2_ref, b2_ref, y_ref):
    h = h2_ref[...]
    a = (_silu(jnp.dot(h, wsg_ref[...], preferred_element_type=F32))
         * jnp.dot(h, wsu_ref[...], preferred_element_type=F32))
    shared = jnp.dot(a.astype(BF16), wsd_ref[...], preferred_element_type=F32)
    f = routed_ref[...] + shared
    y_ref[...] = _layer_norm(ALPHA * x1_ref[...] + gate2_ref[0] * f) * g2_ref[...] + b2_ref[...]


def final_out(x1, routed, h2, gate2, wsg, wsu, wsd, ln_g, ln_b, tiles_per_group):
    m = x1.shape[0]
    tm = ROW_TILE
    r = gate2.shape[1]
    row = lambda w: pl.BlockSpec((tm, w), lambda i: (i, 0))
    mod = pl.BlockSpec((1, r, D_MODEL), lambda i: (i // tiles_per_group, 0, 0))
    return pl.pallas_call(
        _final_kernel,
        grid=(m // tm,),
        in_specs=[row(D_MODEL), row(D_MODEL), row(D_MODEL), mod,
                  _resident((D_MODEL, D_SHARED)), _resident((D_MODEL, D_SHARED)), _resident((D_SHARED, D_MODEL)),
                  _resident((1, D_MODEL)), _resident((1, D_MODEL))],
        out_specs=row(D_MODEL),
        out_shape=jax.ShapeDtypeStruct((m, D_MODEL), F32),
        compiler_params=_params("arbitrary"),
        name="final_out",
    )(x1, routed, h2, gate2, wsg, wsu, wsd, ln_g.reshape(1, -1), ln_b.reshape(1, -1))


def _select_blocks(q, means, n_past, ks):
    scores = jnp.einsum('nqhd,nbhd->nqhb', q.astype(F32), means, precision=HIGHEST)
    eligible = jnp.arange(means.shape[1]) < n_past
    scores = jnp.where(eligible, scores, -jnp.inf)
    _, idx = lax.top_k(scores, ks)
    return idx, idx < n_past


def _moba_attend(q, q_pos, own_k, own_v, own_pos, rel_bias, sel):
    qf = q.astype(F32) * (HD_A ** -0.5)
    rel = q_pos[:, None] - own_pos[None, :]
    own_logit = jnp.einsum('nqhd,nkhd->nqhk', qf, own_k.astype(F32))
    own_logit = own_logit + rel_bias[t5_bucket(rel)].astype(F32).transpose(0, 2, 1)[None]
    own_logit = jnp.where((rel >= 0)[None, :, None, :], own_logit, -jnp.inf)
    sel_k, sel_v, sel_pos, valid = sel
    h_i = jnp.arange(H_A)[None, None, :, None, None]
    sel_logit = jnp.einsum('nqhd,nqhskd->nqhsk', qf, sel_k.astype(F32))
    rel_s = q_pos[None, :, None, None, None] - sel_pos
    sel_logit = sel_logit + rel_bias[t5_bucket(rel_s), h_i].astype(F32)
    sel_logit = jnp.where(valid[..., None], sel_logit, -jnp.inf)
    n, nq, nh, ks, nb = sel_logit.shape
    logits = jnp.concatenate([sel_logit.reshape(n, nq, nh, ks * nb), own_logit], axis=-1)
    p = jax.nn.softmax(logits, axis=-1)
    p_sel = p[..., :ks * nb].reshape(n, nq, nh, ks, nb)
    p_own = p[..., ks * nb:]
    o = (jnp.einsum('nqhsk,nqhskd->nqhd', p_sel, sel_v.astype(F32))
         + jnp.einsum('nqhk,nkhd->nqhd', p_own, own_v.astype(F32)))
    return o.astype(q.dtype)


def moba_sample(q, k_new, v_new, cache_k, cache_v, page_table, layer, rel_bias):
    n, t = q.shape[:2]
    past_len = page_table.shape[1] * PAGE_SIZE
    b_own = past_len // MOBA_BLOCK
    own_start = b_own * MOBA_BLOCK
    q_pos = past_len + jnp.arange(t)
    p0, p1 = own_start // PAGE_SIZE, past_len // PAGE_SIZE
    n_own = (p1 - p0) * PAGE_SIZE
    own_pages = page_table[:, p0:p1]
    own_k = jnp.concatenate([cache_k[layer, own_pages].reshape(n, n_own, H_A, HD_A), k_new], axis=1)
    own_v = jnp.concatenate([cache_v[layer, own_pages].reshape(n, n_own, H_A, HD_A), v_new], axis=1)
    own_pos = jnp.concatenate([jnp.arange(own_start, past_len), q_pos])
    ks = min(MOBA_TOPK, b_own)
    n_blk_pages = b_own * (MOBA_BLOCK // PAGE_SIZE)

    def seq_block_means(pt):
        rows = cache_k[layer, pt[:n_blk_pages]].astype(F32)
        return jnp.mean(rows.reshape(b_own, MOBA_BLOCK, H_A, HD_A), axis=1)

    means = lax.map(seq_block_means, page_table)
    idx, valid = _select_blocks(q, means, b_own, ks)
    rows = idx[..., None] * MOBA_BLOCK + jnp.arange(MOBA_BLOCK)
    n_i = jnp.arange(n)[:, None, None, None, None]
    h_i = jnp.arange(H_A)[None, None, :, None, None]
    phys = page_table[n_i, rows // PAGE_SIZE]
    off = rows % PAGE_SIZE
    sel = (cache_k[layer, phys, off, h_i], cache_v[layer, phys, off, h_i], rows, valid)
    return _moba_attend(q, q_pos, own_k, own_v, own_pos, rel_bias, sel).reshape(n, t, ATT_W)


def _dispatch(idx_t, gate_t):
    m = idx_t.shape[1]
    bm = MOE_ROW_TILE
    n_pairs = m * TOP_K
    flat_e = idx_t.T.reshape(-1)
    flat_tok = jnp.repeat(jnp.arange(m, dtype=jnp.int32), TOP_K)
    flat_w = gate_t.T.reshape(-1)
    order = jnp.argsort(flat_e)
    se, stok, sw = flat_e[order], flat_tok[order], flat_w[order]
    counts = jnp.bincount(flat_e, length=N_EXPERTS)
    padded = (counts + bm - 1) // bm * bm
    pad_end = jnp.cumsum(padded)
    pad_start = pad_end - padded
    start = jnp.cumsum(counts) - counts
    dest = pad_start[se] + jnp.arange(n_pairs) - start[se]
    n_blocks = -(-(n_pairs + N_EXPERTS * (bm - 1)) // bm)
    n_rows = n_blocks * bm
    buf_tok = jnp.zeros((n_rows,), jnp.int32).at[dest].set(stok)
    buf_w = jnp.zeros((n_rows,), F32).at[dest].set(sw)
    blk_e = jnp.minimum(jnp.searchsorted(pad_end, jnp.arange(n_blocks) * bm, side='right'),
                        N_EXPERTS - 1).astype(jnp.int32)
    n_used = (pad_end[-1] // bm).astype(jnp.int32).reshape(1)
    return buf_tok, buf_w, blk_e, n_used


def _expand_rows(v, reps):
    return jnp.repeat(v, reps, axis=0).reshape(-1, ROW_TILE, v.shape[-1])


def kernel(x_prompt, x_sample, cache_k, cache_v, state_ret, page_table, c_prompt, c_sample, w_ada, b_ada, w_in, rel_bias, ret_norm_g, ret_norm_b, w_branch_a, w_branch_b, w_out, ln1_g, ln1_b, w_router, router_bias, w_gate, w_up, w_down, w_shared_gate, w_shared_up, w_shared_down, ln2_g, ln2_b):
    assert DEPTH == 1
    l = 0
    nb, s, d = x_prompt.shape
    ns, t, _ = x_sample.shape
    past_len = page_table.shape[1] * PAGE_SIZE
    mp, ms = nb * s, ns * t

    c_all = jnp.concatenate([c_prompt, c_sample], axis=0)
    n_c = c_all.shape[0]
    c_all = jnp.pad(c_all, ((0, -n_c % 8), (0, 0)))
    mod = ada_modulation(c_all, w_ada[l], b_ada[l])
    mods_p = [mod[:nb, i * d:(i + 1) * d].reshape(nb, 1, d) for i in range(6)]
    mods_s = [_expand_rows(mod[nb:nb + ns, i * d:(i + 1) * d], t) for i in range(6)]
    tiles_p = s // ROW_TILE

    w_in_bf = w_in[l].astype(BF16)
    pp = in_proj(x_prompt.reshape(mp, d), mods_p[0], mods_p[1], w_in_bf, tiles_p)
    ps = in_proj(x_sample.reshape(ms, d), mods_s[0], mods_s[1], w_in_bf, 1)

    att_p = moba_prompt(pp["qa"].reshape(nb, s, ATT_W), pp["kbf"].reshape(nb, s, ATT_W),
                        pp["vbf"].reshape(nb, s, ATT_W), pp["kmean"].reshape(nb, s // MOBA_BLOCK, ATT_W),
                        rel_bias)
    tab_p = _retention_tables(jnp.arange(s), float(RET_CHUNK), RET_CHUNK)
    ret_p, st_p = retention(pp["qr"].reshape(nb, s, -1), pp["kr"].reshape(nb, s, -1),
                            pp["vr"].reshape(nb, s, -1), pp["gr"].reshape(nb, s, -1),
                            jnp.zeros((nb, H_R, DK_R, DV_R), F32), tab_p, ret_norm_g[l], ret_norm_b[l])

    att_s = moba_sample(ps["qa"].reshape(ns, t, H_A, HD_A), ps["ka"].reshape(ns, t, H_A, HD_A),
                        ps["va"].reshape(ns, t, H_A, HD_A), cache_k, cache_v, page_table, l, rel_bias)
    pad_t = lambda a: jnp.pad(a.reshape(ns, t, -1), ((0, 0), (0, RET_CHUNK - t), (0, 0)))
    pos_s = past_len + jnp.arange(RET_CHUNK)
    tab_s = _retention_tables(pos_s, float(t), RET_CHUNK)
    ret_s, st_s = retention(pad_t(ps["qr"]), pad_t(ps["kr"]), pad_t(ps["vr"]), pad_t(ps["gr"]),
                            state_ret[l], tab_s, ret_norm_g[l], ret_norm_b[l])
    ret_s = ret_s[:, :t].reshape(ms, RET_V_W)

    wa, wb, wo = w_branch_a[l].astype(BF16), w_branch_b[l].astype(BF16), w_out[l].astype(BF16)
    x1_p, h2_p, lg_p = mix_out(att_p.reshape(mp, ATT_W), ret_p.reshape(mp, RET_V_W), pp["ga"], pp["gb"],
                               x_prompt.reshape(mp, d), mods_p[2], mods_p[3], mods_p[4], wa, wb, wo,
                               ln1_g[l], ln1_b[l], w_router[l], tiles_p)
    x1_s, h2_s, lg_s = mix_out(att_s.reshape(ms, ATT_W).astype(BF16), ret_s, ps["ga"], ps["gb"],
                               x_sample.reshape(ms, d), mods_s[2], mods_s[3], mods_s[4], wa, wb, wo,
                               ln1_g[l], ln1_b[l], w_router[l], 1)

    h2 = jnp.concatenate([h2_p, h2_s], axis=0)
    idx_t, gate_t = route(jnp.concatenate([lg_p, lg_s], axis=0), router_bias[l])
    buf_tok, buf_w, blk_e, n_used = _dispatch(idx_t, gate_t)
    ys = routed_experts(h2[buf_tok], buf_w.reshape(-1, 1), blk_e, n_used, w_gate[l], w_up[l], w_down[l])
    routed = jnp.zeros((mp + ms, d), F32).at[buf_tok].add(ys)

    wsg, wsu, wsd = (w_shared_gate[l].astype(BF16), w_shared_up[l].astype(BF16),
                     w_shared_down[l].astype(BF16))
    y_p = final_out(x1_p, routed[:mp], h2_p, mods_p[5], wsg, wsu, wsd, ln2_g[l], ln2_b[l], tiles_p)
    y_s = final_out(x1_s, routed[mp:], h2_s, mods_s[5], wsg, wsu, wsd, ln2_g[l], ln2_b[l], 1)

    return (y_p.reshape(nb, s, d), y_s.reshape(ns, t, d),
            pp["ka"].reshape(1, nb, s, H_A, HD_A), pp["va"].reshape(1, nb, s, H_A, HD_A), st_p[None],
            ps["ka"].reshape(1, ns, t, H_A, HD_A), ps["va"].reshape(1, ns, t, H_A, HD_A), st_s[None])
```

```python
import math

import jax
import jax.numpy as jnp
from jax import lax
from jax.experimental import pallas as pl
from jax.experimental.pallas import tpu as pltpu

D_MODEL = 1024
DEPTH = 1
PAGE_SIZE = 128
H_A = 8
HD_A = 64
ATT_W = H_A * HD_A
MOBA_BLOCK = 256
MOBA_TOPK = 3
PAGES_PER_BLOCK = MOBA_BLOCK // PAGE_SIZE
N_BUCKETS = 32
MAX_DISTANCE = 128
H_R = 4
DK_R = 128
DV_R = 128
RET_QK_W = H_R * DK_R
RET_V_W = H_R * DV_R
RET_CHUNK = 128
ROPE_BASE = 10000.0
N_EXPERTS = 256
TOP_K = 8
N_GROUPS = 8
TOPK_GROUPS = 4
GROUP_SIZE = N_EXPERTS // N_GROUPS
D_EXPERT = 256
D_SHARED = 256
ROUTED_SCALE = 2.5
ALPHA = (2.0 * DEPTH) ** 0.25
LN_EPS = 1e-5
D_IN = 3 * ATT_W + 2 * RET_QK_W + 2 * RET_V_W + 2 * D_MODEL

F32 = jnp.float32
BF16 = jnp.bfloat16
NEG_INF = float("-inf")
HIGHEST = lax.Precision.HIGHEST
NT_DIMS = (((1,), (1,)), ((), ()))
TN_DIMS = (((0,), (0,)), ((), ()))

VMEM_LIMIT_BYTES = 56 * 1024 * 1024
ROW_TILE = 512
MOE_ROW_TILE = 256
COMBINE_TILE = 256


def _params(*semantics):
    return pltpu.CompilerParams(dimension_semantics=semantics, vmem_limit_bytes=VMEM_LIMIT_BYTES)


def _layer_norm(x):
    mu = jnp.mean(x, axis=-1, keepdims=True)
    xc = x - mu
    var = jnp.mean(xc * xc, axis=-1, keepdims=True)
    return xc * lax.rsqrt(var + LN_EPS)


def _silu(x):
    return x * jax.nn.sigmoid(x)


def _resident(shape):
    zeros = (0,) * len(shape)
    return pl.BlockSpec(shape, lambda *_: zeros, pipeline_mode=pl.Buffered(1))


def _ada_kernel(c_ref, w_ref, b_ref, o_ref):
    a = _silu(c_ref[...])
    o_ref[...] = jnp.dot(a, w_ref[...], precision=HIGHEST, preferred_element_type=F32) + b_ref[...]


def ada_modulation(c, w_ada, b_ada):
    n = c.shape[0]
    tn = 1536
    return pl.pallas_call(
        _ada_kernel,
        grid=(6 * D_MODEL // tn,),
        in_specs=[pl.BlockSpec((n, D_MODEL), lambda j: (0, 0)),
                  pl.BlockSpec((D_MODEL, tn), lambda j: (0, j)),
                  pl.BlockSpec((1, tn), lambda j: (0, j))],
        out_specs=pl.BlockSpec((n, tn), lambda j: (0, j)),
        out_shape=jax.ShapeDtypeStruct((n, 6 * D_MODEL), F32),
        compiler_params=_params("arbitrary"),
        name="ada_modulation",
    )(c, w_ada, b_ada.reshape(1, -1))


def _in_proj_kernel(x_ref, shift_ref, scale_ref, w_ref,
                    qa_ref, ka_ref, va_ref, kbf_ref, vbf_ref, kmean_ref,
                    qr_ref, kr_ref, vr_ref, gr_ref, ga_ref, gb_ref):
    h = (_layer_norm(x_ref[...]) * (1.0 + scale_ref[0]) + shift_ref[0]).astype(BF16)

    def proj(start, width):
        return jnp.dot(h, w_ref[:, start:start + width], preferred_element_type=F32)

    qa_ref[...] = proj(0, ATT_W)
    ka = proj(ATT_W, ATT_W)
    ka_ref[...] = ka
    kbf_ref[...] = ka.astype(BF16)
    n_blk = ka.shape[0] // MOBA_BLOCK
    for b in range(n_blk):
        kmean_ref[b] = jnp.mean(ka[b * MOBA_BLOCK:(b + 1) * MOBA_BLOCK], axis=0, keepdims=True)
    va = proj(2 * ATT_W, ATT_W)
    va_ref[...] = va
    vbf_ref[...] = va.astype(BF16)
    off = 3 * ATT_W
    qr_ref[...] = proj(off, RET_QK_W)
    kr_ref[...] = proj(off + RET_QK_W, RET_QK_W)
    off += 2 * RET_QK_W
    vr_ref[...] = proj(off, RET_V_W).astype(BF16)
    gr_ref[...] = proj(off + RET_V_W, RET_V_W)
    off += 2 * RET_V_W
    ga_ref[...] = proj(off, D_MODEL)
    gb_ref[...] = proj(off + D_MODEL, D_MODEL)


def in_proj(x, shift, scale, w_in_bf16, tiles_per_group):
    m = x.shape[0]
    tm = ROW_TILE
    r = shift.shape[1]
    row = lambda w: pl.BlockSpec((tm, w), lambda i: (i, 0))
    mod = pl.BlockSpec((1, r, D_MODEL), lambda i: (i // tiles_per_group, 0, 0))
    widths = dict(qa=ATT_W, ka=ATT_W, va=ATT_W, kbf=ATT_W, vbf=ATT_W, qr=RET_QK_W, kr=RET_QK_W,
                  vr=RET_V_W, gr=RET_V_W, ga=D_MODEL, gb=D_MODEL)
    dtypes = dict(kbf=BF16, vbf=BF16, vr=BF16)
    names = ["qa", "ka", "va", "kbf", "vbf", "kmean", "qr", "kr", "vr", "gr", "ga", "gb"]
    out_specs, out_shapes = [], []
    for nm in names:
        if nm == "kmean":
            out_specs.append(pl.BlockSpec((tm // MOBA_BLOCK, 1, ATT_W), lambda i: (i, 0, 0)))
            out_shapes.append(jax.ShapeDtypeStruct((m // MOBA_BLOCK, 1, ATT_W), F32))
        else:
            out_specs.append(row(widths[nm]))
            out_shapes.append(jax.ShapeDtypeStruct((m, widths[nm]), dtypes.get(nm, F32)))
    outs = pl.pallas_call(
        _in_proj_kernel,
        grid=(m // tm,),
        in_specs=[row(D_MODEL), mod, mod, _resident((D_MODEL, D_IN))],
        out_specs=out_specs,
        out_shape=out_shapes,
        compiler_params=_params("arbitrary"),
        name="in_proj",
    )(x, shift, scale, w_in_bf16)
    return dict(zip(names, outs))


def t5_bucket(rel):
    n = jnp.maximum(rel, 0)
    max_exact = N_BUCKETS // 2
    ratio = jnp.log(jnp.maximum(n, max_exact).astype(F32) / max_exact) / math.log(MAX_DISTANCE / max_exact)
    large = jnp.minimum(max_exact + (ratio * (N_BUCKETS - max_exact)).astype(jnp.int32), N_BUCKETS - 1)
    return jnp.where(n < max_exact, n, large)


def _bias_from_buckets(bucket, rel_bias_ref, head):
    out = jnp.zeros(bucket.shape, F32)
    for b in range(N_BUCKETS):
        out = jnp.where(bucket == b, rel_bias_ref[b, head], out)
    return out


def _bias_rows_from_buckets(bucket, relb_rows):
    out = jnp.zeros(bucket.shape, F32)
    for b in range(N_BUCKETS):
        out = jnp.where(bucket == b, relb_rows[:, b:b + 1], out)
    return out


def _top_block_ids(scores, n_eligible, n_pick):
    n_blocks = scores.shape[1]
    blk = lax.broadcasted_iota(jnp.int32, scores.shape, 1)
    sc = jnp.where(blk < n_eligible, scores, NEG_INF)
    picks = []
    for _ in range(n_pick):
        best = jnp.max(sc, axis=1, keepdims=True)
        first = jnp.min(jnp.where(sc == best, blk, n_blocks), axis=1, keepdims=True)
        picks.append(jnp.where(first < n_eligible, first, -1))
        sc = jnp.where(blk == first, NEG_INF, sc)
    return picks


def _picked(picks, block_id):
    keep = picks[0] == block_id
    for p in picks[1:]:
        keep = jnp.logical_or(keep, p == block_id)
    return keep


def _moba_prompt_kernel(relb_ref, q_ref, k_ref, v_ref, kmean_ref, bkt_own_ref, bkt_prev_ref,
                        o_ref, bias_own, bias_prev):
    n_i = pl.program_id(0)
    kb = pl.program_id(1)
    blk_rows = MOBA_BLOCK

    @pl.when(jnp.logical_and(n_i == 0, kb == 0))
    def _():
        for h in range(H_A):
            bias_own[h] = _bias_from_buckets(bkt_own_ref[...], relb_ref, h)
            bias_prev[h] = _bias_from_buckets(bkt_prev_ref[...], relb_ref, h)

    q = q_ref[0]
    kmean = kmean_ref[0]
    row = lax.broadcasted_iota(jnp.int32, (blk_rows, blk_rows), 0)
    col = lax.broadcasted_iota(jnp.int32, (blk_rows, blk_rows), 1)
    causal = row >= col
    own_start = pl.multiple_of(kb * blk_rows, blk_rows)
    prev_blk = jnp.maximum(kb - 1, 0)
    prev_start = pl.multiple_of(prev_blk * blk_rows, blk_rows)
    outs = []
    for h in range(H_A):
        lanes = slice(h * HD_A, (h + 1) * HD_A)
        qh = q[:, lanes]
        picks = _top_block_ids(lax.dot_general(qh, kmean[:, lanes], NT_DIMS, precision=HIGHEST,
                                               preferred_element_type=F32), kb, MOBA_TOPK)
        qs = (qh * (HD_A ** -0.5)).astype(BF16)

        def attend(k_start, bias, keep, carry):
            m, l, acc = carry
            kh = k_ref[0, pl.ds(k_start, blk_rows), lanes]
            vh = v_ref[0, pl.ds(k_start, blk_rows), lanes]
            s = lax.dot_general(qs, kh, NT_DIMS, preferred_element_type=F32) + bias
            s = jnp.where(keep, s, NEG_INF)
            m_new = jnp.maximum(m, jnp.max(s, axis=1, keepdims=True))
            alpha = jnp.exp(m - m_new)
            p = jnp.exp(s - m_new)
            l = alpha * l + jnp.sum(p, axis=1, keepdims=True)
            acc = alpha * acc + jnp.dot(p.astype(BF16), vh, preferred_element_type=F32)
            return m_new, l, acc

        s0 = lax.dot_general(qs, k_ref[0, pl.ds(own_start, blk_rows), lanes], NT_DIMS,
                             preferred_element_type=F32) + bias_own[h]
        s0 = jnp.where(causal, s0, NEG_INF)
        m0 = jnp.max(s0, axis=1, keepdims=True)
        p0 = jnp.exp(s0 - m0)
        carry = (m0, jnp.sum(p0, axis=1, keepdims=True),
                 jnp.dot(p0.astype(BF16), v_ref[0, pl.ds(own_start, blk_rows), lanes],
                         preferred_element_type=F32))
        carry = attend(prev_start, bias_prev[h], _picked(picks, prev_blk), carry)
        far_bias = relb_ref[N_BUCKETS - 1, h]

        def far_body(j, c):
            return attend(pl.multiple_of(j * blk_rows, blk_rows), far_bias, _picked(picks, j), c)

        m, l, acc = lax.fori_loop(0, jnp.maximum(kb - 1, 0), far_body, carry)
        outs.append(acc / l)
    o_ref[0] = jnp.concatenate(outs, axis=-1).astype(o_ref.dtype)


def moba_prompt(q, kbf, vbf, kmean, rel_bias):
    n, s, _ = q.shape
    nb = s // MOBA_BLOCK
    i = jnp.arange(MOBA_BLOCK)
    rel_own = i[:, None] - i[None, :]
    bkt_own = t5_bucket(rel_own).astype(jnp.int32)
    bkt_prev = t5_bucket(rel_own + MOBA_BLOCK).astype(jnp.int32)
    grid_spec = pltpu.PrefetchScalarGridSpec(
        num_scalar_prefetch=1,
        grid=(n, nb),
        in_specs=[pl.BlockSpec((1, MOBA_BLOCK, ATT_W), lambda a, b, *_: (a, b, 0)),
                  pl.BlockSpec((1, s, ATT_W), lambda a, b, *_: (a, 0, 0)),
                  pl.BlockSpec((1, s, ATT_W), lambda a, b, *_: (a, 0, 0)),
                  pl.BlockSpec((1, nb, ATT_W), lambda a, b, *_: (a, 0, 0)),
                  pl.BlockSpec((MOBA_BLOCK, MOBA_BLOCK), lambda a, b, *_: (0, 0)),
                  pl.BlockSpec((MOBA_BLOCK, MOBA_BLOCK), lambda a, b, *_: (0, 0))],
        out_specs=pl.BlockSpec((1, MOBA_BLOCK, ATT_W), lambda a, b, *_: (a, b, 0)),
        scratch_shapes=[pltpu.VMEM((H_A, MOBA_BLOCK, MOBA_BLOCK), F32),
                        pltpu.VMEM((H_A, MOBA_BLOCK, MOBA_BLOCK), F32)],
    )
    return pl.pallas_call(
        _moba_prompt_kernel,
        grid_spec=grid_spec,
        out_shape=jax.ShapeDtypeStruct((n, s, ATT_W), BF16),
        compiler_params=_params("arbitrary", "arbitrary"),
        name="moba_prompt",
    )(rel_bias, q, kbf, vbf, kmean, bkt_own, bkt_prev)


def _moba_sample_kernel(pt_ref, q_ref, knew_ref, vnew_ref, relb_rows_ref, bkt_last_ref, bkt_own_ref,
                        ck_hbm, cv_hbm, o_ref,
                        kbuf, vbuf, kown, vown, logit, prob, means, bias_last, bias_own, sems):
    seq = pl.program_id(0)
    n_seq = pl.num_programs(0)
    n_pages = kbuf.shape[0]
    n_blocks = n_pages // PAGES_PER_BLOCK
    t_new = q_ref.shape[1]
    n_rows = t_new * H_A

    def page_copy(cache, buf, sem_slot, s, j):
        return pltpu.make_async_copy(cache.at[pt_ref[s, j]], buf.at[j], sems.at[sem_slot])

    def start_pages(cache, buf, sem_slot, s):
        for j in range(n_pages):
            page_copy(cache, buf, sem_slot, s, j).start()

    def wait_pages(cache, buf, sem_slot, s):
        for j in range(n_pages):
            page_copy(cache, buf, sem_slot, s, j).wait()

    @pl.when(seq == 0)
    def _():
        start_pages(ck_hbm, kbuf, 0, 0)
        start_pages(cv_hbm, vbuf, 1, 0)
        kown[...] = jnp.zeros_like(kown)
        vown[...] = jnp.zeros_like(vown)
        relb_rows = relb_rows_ref[...]
        bias_last[...] = _bias_rows_from_buckets(bkt_last_ref[...], relb_rows)
        bias_own[...] = _bias_rows_from_buckets(bkt_own_ref[...], relb_rows)

    q = q_ref[0]
    head_mask = (lax.broadcasted_iota(jnp.int32, (H_A, ATT_W), 1) // HD_A
                 == lax.broadcasted_iota(jnp.int32, (H_A, ATT_W), 0))
    qbd = jnp.concatenate([jnp.where(head_mask, jnp.broadcast_to(q[t:t + 1], (H_A, ATT_W)), 0.0)
                           for t in range(t_new)], axis=0)
    qs = (qbd * (HD_A ** -0.5)).astype(BF16)
    kown[0:t_new, :] = knew_ref[0]
    vown[0:t_new, :] = vnew_ref[0]

    wait_pages(ck_hbm, kbuf, 0, seq)

    def k_block(b, carry):
        tot = jnp.zeros((1, ATT_W), F32)
        for pg in range(PAGES_PER_BLOCK):
            j = b * PAGES_PER_BLOCK + pg
            kp = kbuf[j]
            tot = tot + jnp.sum(kp, axis=0, keepdims=True)
            logit[j] = lax.dot_general(qs, kp.astype(BF16), NT_DIMS, preferred_element_type=F32)
        means[pl.ds(b, 1), :] = tot * (1.0 / MOBA_BLOCK)
        return carry

    lax.fori_loop(0, n_blocks, k_block, 0)

    @pl.when(seq + 1 < n_seq)
    def _():
        start_pages(ck_hbm, kbuf, 0, seq + 1)

    scores = lax.dot_general(qbd, means[...], NT_DIMS, precision=HIGHEST, preferred_element_type=F32)
    picks = _top_block_ids(scores, n_blocks, min(MOBA_TOPK, n_blocks))

    far_bias = relb_rows_ref[:, N_BUCKETS - 1:N_BUCKETS]
    own_ok = bkt_own_ref[...] >= 0
    s_own = lax.dot_general(qs, kown[...].astype(BF16), NT_DIMS, preferred_element_type=F32)
    s_own = jnp.where(own_ok, s_own + bias_own[...], NEG_INF)
    mx = s_own
    for j in range(n_pages):
        b, pg = divmod(j, PAGES_PER_BLOCK)
        if b == n_blocks - 1:
            bias = bias_last[:, pg * PAGE_SIZE:(pg + 1) * PAGE_SIZE]
        else:
            bias = far_bias
        s = jnp.where(_picked(picks, b), logit[j] + bias, NEG_INF)
        logit[j] = s
        mx = jnp.maximum(mx, s)
    m = jnp.max(mx, axis=1, keepdims=True)
    p_own = jnp.exp(s_own - m)
    lsum = p_own
    for j in range(n_pages):
        p = jnp.exp(logit[j] - m)
        prob[j] = p.astype(BF16)
        lsum = lsum + p
    l = jnp.sum(lsum, axis=1, keepdims=True)

    wait_pages(cv_hbm, vbuf, 1, seq)

    def v_block(b, acc):
        for pg in range(PAGES_PER_BLOCK):
            j = b * PAGES_PER_BLOCK + pg
            acc = acc + jnp.dot(prob[j], vbuf[j].astype(BF16), preferred_element_type=F32)
        return acc

    acc = lax.fori_loop(0, n_blocks, v_block, jnp.zeros((n_rows, ATT_W), F32))

    @pl.when(seq + 1 < n_seq)
    def _():
        start_pages(cv_hbm, vbuf, 1, seq + 1)

    acc = acc + jnp.dot(p_own.astype(BF16), vown[...].astype(BF16), preferred_element_type=F32)
    acc = acc / l
    for t in range(t_new):
        rows = acc[t * H_A:(t + 1) * H_A]
        o_ref[0, t:t + 1, :] = jnp.sum(jnp.where(head_mask, rows, 0.0), axis=0, keepdims=True)


def moba_sample(q, k_new, v_new, cache_k, cache_v, page_table, rel_bias):
    n, t, _ = q.shape
    n_pages = page_table.shape[1]
    assert n_pages % PAGES_PER_BLOCK == 0 and n_pages >= PAGES_PER_BLOCK
    n_keys = n_pages * PAGE_SIZE
    n_rows = t * H_A
    t_of_row = jnp.arange(n_rows) // H_A
    relb_rows = jnp.tile(rel_bias.T, (t, 1))
    bkt_last = t5_bucket(MOBA_BLOCK + t_of_row[:, None] - jnp.arange(MOBA_BLOCK)[None, :]).astype(jnp.int32)
    cols = jnp.arange(PAGE_SIZE)[None, :]
    rel_own = t_of_row[:, None] - cols
    bkt_own = jnp.where((rel_own >= 0) & (cols < t), t5_bucket(rel_own), -1).astype(jnp.int32)
    seq_spec = pl.BlockSpec((1, t, ATT_W), lambda s, *_: (s, 0, 0))
    const = lambda shape: pl.BlockSpec(shape, lambda s, *_: (0,) * len(shape))
    grid_spec = pltpu.PrefetchScalarGridSpec(
        num_scalar_prefetch=1,
        grid=(n,),
        in_specs=[seq_spec, seq_spec, seq_spec,
                  const((n_rows, N_BUCKETS)), const((n_rows, MOBA_BLOCK)), const((n_rows, PAGE_SIZE)),
                  pl.BlockSpec(memory_space=pl.ANY), pl.BlockSpec(memory_space=pl.ANY)],
        out_specs=seq_spec,
        scratch_shapes=[pltpu.VMEM((n_pages, PAGE_SIZE, ATT_W), F32),
                        pltpu.VMEM((n_pages, PAGE_SIZE, ATT_W), F32),
                        pltpu.VMEM((PAGE_SIZE, ATT_W), F32),
                        pltpu.VMEM((PAGE_SIZE, ATT_W), F32),
                        pltpu.VMEM((n_pages, n_rows, PAGE_SIZE), F32),
                        pltpu.VMEM((n_pages, n_rows, PAGE_SIZE), BF16),
                        pltpu.VMEM((n_pages // PAGES_PER_BLOCK, ATT_W), F32),
                        pltpu.VMEM((n_rows, MOBA_BLOCK), F32),
                        pltpu.VMEM((n_rows, PAGE_SIZE), F32),
                        pltpu.SemaphoreType.DMA((2,))],
    )
    return pl.pallas_call(
        _moba_sample_kernel,
        grid_spec=grid_spec,
        out_shape=jax.ShapeDtypeStruct((n, t, ATT_W), F32),
        compiler_params=_params("arbitrary"),
        name="moba_sample",
    )(page_table, q, k_new, v_new, relb_rows, bkt_last, bkt_own, cache_k, cache_v)


def _swap_pairs(x):
    lane = lax.broadcasted_iota(jnp.int32, x.shape, 1)
    width = x.shape[1]
    return jnp.where(lane % 2 == 0, pltpu.roll(x, width - 1, 1), pltpu.roll(x, 1, 1))


def _retention_kernel(q_ref, k_ref, v_ref, g_ref, cos_ref, sin_ref, decay_ref, cross_ref, kw_ref,
                      sdec_ref, ng_ref, nb_ref, st_in_ref, y_ref, st_out_ref, st):
    c = pl.program_id(1)

    @pl.when(c == 0)
    def _():
        st[...] = st_in_ref[0]

    cos = cos_ref[...]
    sin = sin_ref[...]
    q = q_ref[0]
    k = k_ref[0]
    v = v_ref[0]
    g = g_ref[0]
    for h in range(H_R):
        lanes = slice(h * DK_R, (h + 1) * DK_R)
        qh = q[:, lanes]
        kh = k[:, lanes]
        qh = (qh * cos + _swap_pairs(qh) * sin).astype(BF16)
        kh = (kh * cos + _swap_pairs(kh) * sin) * (DK_R ** -0.5)
        vh = v[:, lanes]
        state = st[h]
        scores = lax.dot_general(qh, kh.astype(BF16), NT_DIMS, preferred_element_type=F32) * decay_ref[h]
        inner = jnp.dot(scores.astype(BF16), vh, preferred_element_type=F32)
        cross = jnp.dot(qh, state.astype(BF16), preferred_element_type=F32) * cross_ref[:, lanes]
        kd = (kh * kw_ref[:, lanes]).astype(BF16)
        st[h] = sdec_ref[h] * state + lax.dot_general(kd, vh, TN_DIMS, preferred_element_type=F32)
        o = inner + cross
        o = _layer_norm(o) * ng_ref[:, lanes] + nb_ref[:, lanes]
        y_ref[0, :, lanes] = (_silu(g[:, lanes]) * o).astype(y_ref.dtype)

    @pl.when(c == pl.num_programs(1) - 1)
    def _():
        st_out_ref[0] = st[...]


def _retention_tables(pos, n_valid, chunk):
    theta = 1.0 / (ROPE_BASE ** jnp.linspace(0.0, 1.0, DK_R // 2, dtype=F32))
    ang = pos.astype(F32)[:, None] * theta[None, :]
    cos = jnp.repeat(jnp.cos(ang), 2, axis=1)
    sin = jnp.repeat(jnp.sin(ang), 2, axis=1) * jnp.tile(jnp.array([-1.0, 1.0], F32), DK_R // 2)[None, :]
    log_g = jnp.log1p(-jnp.exp2(-5.0 - jnp.arange(H_R, dtype=F32)))
    i = jnp.arange(chunk, dtype=F32)
    valid = i < n_valid
    diff = i[:, None] - i[None, :]
    decay = jnp.where(diff >= 0, jnp.exp(jnp.maximum(diff, 0.0)[None] * log_g[:, None, None]), 0.0)
    decay = jnp.where(valid[None, :, None] & valid[None, None, :], decay, 0.0)
    cross = jnp.where(valid[:, None], jnp.exp((i + 1.0)[:, None] * log_g[None, :]), 0.0)
    kw = jnp.where(valid[:, None], jnp.exp((n_valid - 1.0 - i)[:, None] * log_g[None, :]), 0.0)
    sdec = jnp.exp(n_valid * log_g)
    return dict(cos=cos, sin=sin, decay=decay,
                cross=jnp.repeat(cross, DV_R, axis=1), kw=jnp.repeat(kw, DK_R, axis=1),
                sdec=jnp.broadcast_to(sdec[:, None, None], (H_R, 1, DV_R)))


def retention(q, k, v, g, state0, tables, norm_g, norm_b):
    n, s, _ = q.shape
    chunk = tables["decay"].shape[1]
    n_chunks = s // chunk
    seq = pl.BlockSpec((1, chunk, RET_QK_W), lambda a, c: (a, c, 0))
    per_chunk = pl.BlockSpec((chunk, DK_R), lambda a, c: (c, 0))
    per_row = pl.BlockSpec((chunk, RET_QK_W), lambda a, c: (0, 0))
    st_spec = pl.BlockSpec((1, H_R, DK_R, DV_R), lambda a, c: (a, 0, 0, 0))
    vec = pl.BlockSpec((1, RET_V_W), lambda a, c: (0, 0))
    return pl.pallas_call(
        _retention_kernel,
        grid=(n, n_chunks),
        in_specs=[seq, seq, seq, seq, per_chunk, per_chunk,
                  pl.BlockSpec((H_R, chunk, chunk), lambda a, c: (0, 0, 0)),
                  per_row, per_row,
                  pl.BlockSpec((H_R, 1, DV_R), lambda a, c: (0, 0, 0)),
                  vec, vec, st_spec],
        out_specs=[seq, st_spec],
        out_shape=[jax.ShapeDtypeStruct((n, s, RET_V_W), BF16),
                   jax.ShapeDtypeStruct((n, H_R, DK_R, DV_R), F32)],
        scratch_shapes=[pltpu.VMEM((H_R, DK_R, DV_R), F32)],
        compiler_params=_params("arbitrary", "arbitrary"),
        name="retention",
    )(q, k, v, g, tables["cos"], tables["sin"], tables["decay"], tables["cross"], tables["kw"],
      tables["sdec"], norm_g.reshape(1, -1), norm_b.reshape(1, -1), state0)


def _mix_kernel(att_ref, ret_ref, ga_ref, gb_ref, x_ref, gate1_ref, shift2_ref, scale2_ref,
                wa_ref, wb_ref, wo_ref, g1_ref, b1_ref, wr_ref, x1_ref, h2_ref, logit_ref):
    ya = jnp.dot(att_ref[...], wa_ref[...], preferred_element_type=F32)
    yb = jnp.dot(ret_ref[...], wb_ref[...], preferred_element_type=F32)
    mixed_in = jax.nn.sigmoid(ga_ref[...]) * ya + jax.nn.sigmoid(gb_ref[...]) * yb
    mixed = jnp.dot(mixed_in.astype(BF16), wo_ref[...], preferred_element_type=F32)
    x1 = _layer_norm(ALPHA * x_ref[...] + gate1_ref[0] * mixed) * g1_ref[...] + b1_ref[...]
    x1_ref[...] = x1
    h2 = _layer_norm(x1) * (1.0 + scale2_ref[0]) + shift2_ref[0]
    h2_ref[...] = h2
    logit_ref[...] = jnp.dot(h2, wr_ref[...], precision=HIGHEST, preferred_element_type=F32)


def mix_out(att, ret, ga, gb, x, gate1, shift2, scale2, wa, wb, wo, ln_g, ln_b, w_router, tiles_per_group):
    m = x.shape[0]
    tm = ROW_TILE
    r = gate1.shape[1]
    row = lambda w: pl.BlockSpec((tm, w), lambda i: (i, 0))
    mod = pl.BlockSpec((1, r, D_MODEL), lambda i: (i // tiles_per_group, 0, 0))
    return pl.pallas_call(
        _mix_kernel,
        grid=(m // tm,),
        in_specs=[row(ATT_W), row(RET_V_W), row(D_MODEL), row(D_MODEL), row(D_MODEL), mod, mod, mod,
                  _resident((ATT_W, D_MODEL)), _resident((RET_V_W, D_MODEL)), _resident((D_MODEL, D_MODEL)),
                  _resident((1, D_MODEL)), _resident((1, D_MODEL)), _resident((D_MODEL, N_EXPERTS))],
        out_specs=[row(D_MODEL), row(D_MODEL), row(N_EXPERTS)],
        out_shape=[jax.ShapeDtypeStruct((m, D_MODEL), F32), jax.ShapeDtypeStruct((m, D_MODEL), F32),
                   jax.ShapeDtypeStruct((m, N_EXPERTS), F32)],
        compiler_params=_params("arbitrary"),
        name="mix_out",
    )(att, ret, ga, gb, x, gate1, shift2, scale2, wa, wb, wo, ln_g.reshape(1, -1), ln_b.reshape(1, -1),
      w_router)


def _first_max(x, ids, sentinel):
    best = jnp.max(x, axis=0, keepdims=True)
    first = jnp.min(jnp.where(x == best, ids, sentinel), axis=0, keepdims=True)
    return best, first


def _route_kernel(logit_ref, bias_ref, idx_ref, gate_ref):
    s = jax.nn.sigmoid(logit_ref[...].T)
    sb = s + bias_ref[...]
    tm = s.shape[1]
    gid = lax.broadcasted_iota(jnp.int32, (GROUP_SIZE, tm), 0)
    grp_scores = []
    for gi in range(N_GROUPS):
        grp = sb[gi * GROUP_SIZE:(gi + 1) * GROUP_SIZE]
        top1, first = _first_max(grp, gid, GROUP_SIZE)
        top2 = jnp.max(jnp.where(gid == first, NEG_INF, grp), axis=0, keepdims=True)
        grp_scores.append(top1 + top2)
    gs = jnp.concatenate(grp_scores, axis=0)
    gids = lax.broadcasted_iota(jnp.int32, (N_GROUPS, tm), 0)
    gmask = jnp.zeros((N_GROUPS, tm), jnp.bool_)
    for _ in range(TOPK_GROUPS):
        _, first = _first_max(gs, gids, N_GROUPS)
        hit = gids == first
        gmask = jnp.logical_or(gmask, hit)
        gs = jnp.where(hit, NEG_INF, gs)
    masked = jnp.concatenate(
        [jnp.where(gmask[gi:gi + 1], sb[gi * GROUP_SIZE:(gi + 1) * GROUP_SIZE], NEG_INF)
         for gi in range(N_GROUPS)], axis=0)
    eids = lax.broadcasted_iota(jnp.int32, (N_EXPERTS, tm), 0)
    picks, weights = [], []
    for _ in range(TOP_K):
        _, first = _first_max(masked, eids, N_EXPERTS)
        hit = eids == first
        picks.append(first)
        weights.append(jnp.sum(jnp.where(hit, s, 0.0), axis=0, keepdims=True))
        masked = jnp.where(hit, NEG_INF, masked)
    w = jnp.concatenate(weights, axis=0)
    idx_ref[...] = jnp.concatenate(picks, axis=0)
    gate_ref[...] = w / jnp.sum(w, axis=0, keepdims=True) * ROUTED_SCALE


def route(logits, router_bias):
    m = logits.shape[0]
    tm = ROW_TILE
    return pl.pallas_call(
        _route_kernel,
        grid=(m // tm,),
        in_specs=[pl.BlockSpec((tm, N_EXPERTS), lambda i: (i, 0)),
                  pl.BlockSpec((N_EXPERTS, 1), lambda i: (0, 0))],
        out_specs=[pl.BlockSpec((TOP_K, tm), lambda i: (0, i)), pl.BlockSpec((TOP_K, tm), lambda i: (0, i))],
        out_shape=[jax.ShapeDtypeStruct((TOP_K, m), jnp.int32), jax.ShapeDtypeStruct((TOP_K, m), F32)],
        compiler_params=_params("arbitrary"),
        name="route",
    )(logits, router_bias.reshape(-1, 1))


def _expert_kernel(blk_e_ref, n_used_ref, tok_ref, tok_next_ref, wg_ref, wu_ref, wd_ref, h_hbm,
                   y_ref, xbuf, sems):
    i = pl.program_id(0)
    n_used = n_used_ref[0]
    tm = xbuf.shape[1]
    slot = i % 2

    def row_copy(tok_smem, r, to_slot):
        return pltpu.make_async_copy(h_hbm.at[pl.ds(tok_smem[0, 0, r], 1)],
                                     xbuf.at[to_slot, pl.ds(r, 1)], sems.at[to_slot])

    def start_rows(tok_smem, to_slot):
        def body(r, c):
            row_copy(tok_smem, r, to_slot).start()
            return c
        lax.fori_loop(0, tm, body, 0, unroll=8)

    def wait_rows(tok_smem, to_slot):
        def body(r, c):
            row_copy(tok_smem, r, to_slot).wait()
            return c
        lax.fori_loop(0, tm, body, 0, unroll=8)

    @pl.when(jnp.logical_and(i == 0, n_used > 0))
    def _():
        start_rows(tok_ref, 0)

    @pl.when(i + 1 < n_used)
    def _():
        start_rows(tok_next_ref, 1 - slot)

    @pl.when(i < n_used)
    def _():
        wait_rows(tok_ref, slot)
        x = xbuf[slot].astype(BF16)
        a = (_silu(jnp.dot(x, wg_ref[0].astype(BF16), preferred_element_type=F32))
             * jnp.dot(x, wu_ref[0].astype(BF16), preferred_element_type=F32))
        y_ref[...] = jnp.dot(a.astype(BF16), wd_ref[0].astype(BF16), preferred_element_type=F32)

    @pl.when(i >= n_used)
    def _():
        y_ref[...] = jnp.zeros_like(y_ref)


def routed_experts(h, buf_tok, blk_e, n_used, w_gate, w_up, w_down):
    n_blocks, _, tm = buf_tok.shape
    last = n_blocks - 1
    grid_spec = pltpu.PrefetchScalarGridSpec(
        num_scalar_prefetch=2,
        grid=(n_blocks,),
        in_specs=[pl.BlockSpec((1, 1, tm), lambda i, be, nu: (i, 0, 0), memory_space=pltpu.SMEM),
                  pl.BlockSpec((1, 1, tm), lambda i, be, nu: (jnp.minimum(i + 1, last), 0, 0),
                               memory_space=pltpu.SMEM),
                  pl.BlockSpec((1, D_MODEL, D_EXPERT), lambda i, be, nu: (be[i], 0, 0)),
                  pl.BlockSpec((1, D_MODEL, D_EXPERT), lambda i, be, nu: (be[i], 0, 0)),
                  pl.BlockSpec((1, D_EXPERT, D_MODEL), lambda i, be, nu: (be[i], 0, 0)),
                  pl.BlockSpec(memory_space=pl.ANY)],
        out_specs=pl.BlockSpec((tm, D_MODEL), lambda i, be, nu: (i, 0)),
        scratch_shapes=[pltpu.VMEM((2, tm, D_MODEL), F32), pltpu.SemaphoreType.DMA((2,))],
    )
    return pl.pallas_call(
        _expert_kernel,
        grid_spec=grid_spec,
        out_shape=jax.ShapeDtypeStruct((n_blocks * tm, D_MODEL), F32),
        compiler_params=_params("arbitrary"),
        name="routed_experts",
    )(blk_e, n_used, buf_tok, buf_tok, w_gate, w_up, w_down, h)


def _final_kernel(dest_ref, dest_next_ref, x1_ref, h2_ref, gate_ref, gate2_ref, wsg_ref, wsu_ref, wsd_ref,
                  g2_ref, b2_ref, ys_hbm, y_ref, rbuf, sems):
    i = pl.program_id(0)
    n_steps = pl.num_programs(0)
    tm = x1_ref.shape[0]
    n_gather = TOP_K * tm
    slot = i % 2

    def row_copy(dest_smem, r, to_slot):
        return pltpu.make_async_copy(ys_hbm.at[pl.ds(dest_smem[0, 0, r], 1)],
                                     rbuf.at[to_slot, pl.ds(r, 1)], sems.at[to_slot])

    def start_rows(dest_smem, to_slot):
        def body(r, c):
            row_copy(dest_smem, r, to_slot).start()
            return c
        lax.fori_loop(0, n_gather, body, 0, unroll=8)

    def wait_rows(dest_smem, to_slot):
        def body(r, c):
            row_copy(dest_smem, r, to_slot).wait()
            return c
        lax.fori_loop(0, n_gather, body, 0, unroll=8)

    @pl.when(i == 0)
    def _():
        start_rows(dest_ref, 0)

    @pl.when(i + 1 < n_steps)
    def _():
        start_rows(dest_next_ref, 1 - slot)

    h = h2_ref[...].astype(BF16)
    a = (_silu(jnp.dot(h, wsg_ref[...], preferred_element_type=F32))
         * jnp.dot(h, wsu_ref[...], preferred_element_type=F32))
    f = jnp.dot(a.astype(BF16), wsd_ref[...], preferred_element_type=F32)
    wait_rows(dest_ref, slot)
    gate = gate_ref[...]
    routed = jnp.zeros((tm, D_MODEL), F32)
    for k in range(TOP_K):
        routed = routed + rbuf[slot, k * tm:(k + 1) * tm, :] * gate[:, k:k + 1]
    f = routed + f
    y_ref[...] = _layer_norm(ALPHA * x1_ref[...] + gate2_ref[...] * f) * g2_ref[...] + b2_ref[...]


def final_out(x1, h2, gate, gate2_rows, dest, ys, wsg, wsu, wsd, ln_g, ln_b):
    m = x1.shape[0]
    tm = COMBINE_TILE
    n_steps = m // tm
    row = lambda w: pl.BlockSpec((tm, w), lambda i: (i, 0))
    dest_spec = lambda f: pl.BlockSpec((1, 1, TOP_K * tm), lambda i: (f(i), 0, 0), memory_space=pltpu.SMEM)
    return pl.pallas_call(
        _final_kernel,
        grid=(n_steps,),
        in_specs=[dest_spec(lambda i: i), dest_spec(lambda i: jnp.minimum(i + 1, n_steps - 1)),
                  row(D_MODEL), row(D_MODEL), row(TOP_K), row(D_MODEL),
                  _resident((D_MODEL, D_SHARED)), _resident((D_MODEL, D_SHARED)), _resident((D_SHARED, D_MODEL)),
                  _resident((1, D_MODEL)), _resident((1, D_MODEL)),
                  pl.BlockSpec(memory_space=pl.ANY)],
        out_specs=row(D_MODEL),
        out_shape=jax.ShapeDtypeStruct((m, D_MODEL), F32),
        scratch_shapes=[pltpu.VMEM((2, TOP_K * tm, D_MODEL), F32), pltpu.SemaphoreType.DMA((2,))],
        compiler_params=_params("arbitrary"),
        name="final_out",
    )(dest, dest, x1, h2, gate, gate2_rows, wsg, wsu, wsd, ln_g.reshape(1, -1), ln_b.reshape(1, -1), ys)


def _dispatch(idx_t):
    m = idx_t.shape[1]
    bm = MOE_ROW_TILE
    n_pairs = m * TOP_K
    flat_e = idx_t.T.reshape(-1)
    order = jnp.argsort(flat_e)
    se = flat_e[order]
    counts = jnp.bincount(flat_e, length=N_EXPERTS)
    padded = (counts + bm - 1) // bm * bm
    pad_end = jnp.cumsum(padded)
    pad_start = pad_end - padded
    start = jnp.cumsum(counts) - counts
    dest_sorted = (pad_start[se] + jnp.arange(n_pairs) - start[se]).astype(jnp.int32)
    n_blocks = -(-(n_pairs + N_EXPERTS * (bm - 1)) // bm)
    n_rows = n_blocks * bm
    buf_tok = jnp.zeros((n_rows,), jnp.int32).at[dest_sorted].set((order // TOP_K).astype(jnp.int32))
    dest = jnp.zeros((n_pairs,), jnp.int32).at[order].set(dest_sorted)
    blk_e = jnp.minimum(jnp.searchsorted(pad_end, jnp.arange(n_blocks) * bm, side='right'),
                        N_EXPERTS - 1).astype(jnp.int32)
    n_used = (pad_end[-1] // bm).astype(jnp.int32).reshape(1)
    tc = COMBINE_TILE
    dest_tiles = dest.reshape(m // tc, tc, TOP_K).transpose(0, 2, 1).reshape(m // tc, 1, TOP_K * tc)
    return buf_tok.reshape(n_blocks, 1, bm), blk_e, n_used, dest_tiles


def _expand_rows(v, reps):
    return jnp.repeat(v, reps, axis=0).reshape(-1, ROW_TILE, v.shape[-1])


def kernel(x_prompt, x_sample, cache_k, cache_v, state_ret, page_table, c_prompt, c_sample, w_ada, b_ada, w_in, rel_bias, ret_norm_g, ret_norm_b, w_branch_a, w_branch_b, w_out, ln1_g, ln1_b, w_router, router_bias, w_gate, w_up, w_down, w_shared_gate, w_shared_up, w_shared_down, ln2_g, ln2_b):
    assert DEPTH == 1
    l = 0
    nb, s, d = x_prompt.shape
    ns, t, _ = x_sample.shape
    past_len = page_table.shape[1] * PAGE_SIZE
    mp, ms = nb * s, ns * t

    c_all = jnp.concatenate([c_prompt, c_sample], axis=0)
    n_c = c_all.shape[0]
    c_all = jnp.pad(c_all, ((0, -n_c % 8), (0, 0)))
    mod = ada_modulation(c_all, w_ada[l], b_ada[l])
    mods_p = [mod[:nb, i * d:(i + 1) * d].reshape(nb, 1, d) for i in range(6)]
    mods_s = [_expand_rows(mod[nb:nb + ns, i * d:(i + 1) * d], t) for i in range(6)]
    tiles_p = s // ROW_TILE

    w_in_bf = w_in[l].astype(BF16)
    pp = in_proj(x_prompt.reshape(mp, d), mods_p[0], mods_p[1], w_in_bf, tiles_p)
    ps = in_proj(x_sample.reshape(ms, d), mods_s[0], mods_s[1], w_in_bf, 1)

    att_p = moba_prompt(pp["qa"].reshape(nb, s, ATT_W), pp["kbf"].reshape(nb, s, ATT_W),
                        pp["vbf"].reshape(nb, s, ATT_W), pp["kmean"].reshape(nb, s // MOBA_BLOCK, ATT_W),
                        rel_bias)
    tab_p = _retention_tables(jnp.arange(s), float(RET_CHUNK), RET_CHUNK)
    ret_p, st_p = retention(pp["qr"].reshape(nb, s, -1), pp["kr"].reshape(nb, s, -1),
                            pp["vr"].reshape(nb, s, -1), pp["gr"].reshape(nb, s, -1),
                            jnp.zeros((nb, H_R, DK_R, DV_R), F32), tab_p, ret_norm_g[l], ret_norm_b[l])

    n_pool = cache_k.shape[1]
    att_s = moba_sample(ps["qa"].reshape(ns, t, ATT_W), ps["ka"].reshape(ns, t, ATT_W),
                        ps["va"].reshape(ns, t, ATT_W), cache_k[l].reshape(n_pool, PAGE_SIZE, ATT_W),
                        cache_v[l].reshape(n_pool, PAGE_SIZE, ATT_W), page_table, rel_bias)
    pad_t = lambda a: jnp.pad(a.reshape(ns, t, -1), ((0, 0), (0, RET_CHUNK - t), (0, 0)))
    pos_s = past_len + jnp.arange(RET_CHUNK)
    tab_s = _retention_tables(pos_s, float(t), RET_CHUNK)
    ret_s, st_s = retention(pad_t(ps["qr"]), pad_t(ps["kr"]), pad_t(ps["vr"]), pad_t(ps["gr"]),
                            state_ret[l], tab_s, ret_norm_g[l], ret_norm_b[l])
    ret_s = ret_s[:, :t].reshape(ms, RET_V_W)

    wa, wb, wo = w_branch_a[l].astype(BF16), w_branch_b[l].astype(BF16), w_out[l].astype(BF16)
    x1_p, h2_p, lg_p = mix_out(att_p.reshape(mp, ATT_W), ret_p.reshape(mp, RET_V_W), pp["ga"], pp["gb"],
                               x_prompt.reshape(mp, d), mods_p[2], mods_p[3], mods_p[4], wa, wb, wo,
                               ln1_g[l], ln1_b[l], w_router[l], tiles_p)
    x1_s, h2_s, lg_s = mix_out(att_s.reshape(ms, ATT_W).astype(BF16), ret_s, ps["ga"], ps["gb"],
                               x_sample.reshape(ms, d), mods_s[2], mods_s[3], mods_s[4], wa, wb, wo,
                               ln1_g[l], ln1_b[l], w_router[l], 1)

    x1 = jnp.concatenate([x1_p, x1_s], axis=0)
    h2 = jnp.concatenate([h2_p, h2_s], axis=0)
    idx_t, gate_t = route(jnp.concatenate([lg_p, lg_s], axis=0), router_bias[l])
    buf_tok, blk_e, n_used, dest_tiles = _dispatch(idx_t)
    ys = routed_experts(h2, buf_tok, blk_e, n_used, w_gate[l], w_up[l], w_down[l])
    gate2_rows = jnp.concatenate([jnp.broadcast_to(mods_p[5], (nb, s, d)).reshape(mp, d),
                                  mods_s[5].reshape(ms, d)], axis=0)
    wsg, wsu, wsd = (w_shared_gate[l].astype(BF16), w_shared_up[l].astype(BF16),
                     w_shared_down[l].astype(BF16))
    y = final_out(x1, h2, gate_t.T, gate2_rows, dest_tiles, ys, wsg, wsu, wsd, ln2_g[l], ln2_b[l])

    return (y[:mp].reshape(nb, s, d), y[mp:].reshape(ns, t, d),
            pp["ka"].reshape(1, nb, s, H_A, HD_A), pp["va"].reshape(1, nb, s, H_A, HD_A), st_p[None],
            ps["ka"].reshape(1, ns, t, H_A, HD_A), ps["va"].reshape(1, ns, t, H_A, HD_A), st_s[None])
```

```python
import math

import jax
import jax.numpy as jnp
from jax import lax
from jax.experimental import pallas as pl
from jax.experimental.pallas import tpu as pltpu

D_MODEL = 1024
DEPTH = 1
PAGE_SIZE = 128
H_A = 8
HD_A = 64
ATT_W = H_A * HD_A
MOBA_BLOCK = 256
MOBA_TOPK = 3
PAGES_PER_BLOCK = MOBA_BLOCK // PAGE_SIZE
N_BUCKETS = 32
MAX_DISTANCE = 128
H_R = 4
DK_R = 128
DV_R = 128
RET_QK_W = H_R * DK_R
RET_V_W = H_R * DV_R
RET_CHUNK = 128
ROPE_BASE = 10000.0
N_EXPERTS = 256
TOP_K = 8
N_GROUPS = 8
TOPK_GROUPS = 4
GROUP_SIZE = N_EXPERTS // N_GROUPS
D_EXPERT = 256
D_SHARED = 256
ROUTED_SCALE = 2.5
ALPHA = (2.0 * DEPTH) ** 0.25
LN_EPS = 1e-5
D_IN = 3 * ATT_W + 2 * RET_QK_W + 2 * RET_V_W + 2 * D_MODEL

F32 = jnp.float32
BF16 = jnp.bfloat16
NEG_INF = float("-inf")
HIGHEST = lax.Precision.HIGHEST
NT_DIMS = (((1,), (1,)), ((), ()))
TN_DIMS = (((0,), (0,)), ((), ()))

VMEM_LIMIT_BYTES = 56 * 1024 * 1024
ROW_TILE = 512
MOE_ROW_TILE = 256
COMBINE_TILE = 256


def _params(*semantics):
    return pltpu.CompilerParams(dimension_semantics=semantics, vmem_limit_bytes=VMEM_LIMIT_BYTES)


def _layer_norm(x):
    mu = jnp.mean(x, axis=-1, keepdims=True)
    xc = x - mu
    var = jnp.mean(xc * xc, axis=-1, keepdims=True)
    return xc * lax.rsqrt(var + LN_EPS)


def _silu(x):
    return x * jax.nn.sigmoid(x)


def _resident(shape):
    zeros = (0,) * len(shape)
    return pl.BlockSpec(shape, lambda *_: zeros, pipeline_mode=pl.Buffered(1))


def _ada_kernel(c_ref, w_ref, b_ref, o_ref):
    a = _silu(c_ref[...])
    o_ref[...] = jnp.dot(a, w_ref[...], precision=HIGHEST, preferred_element_type=F32) + b_ref[...]


def ada_modulation(c, w_ada, b_ada):
    n = c.shape[0]
    tn = 1536
    return pl.pallas_call(
        _ada_kernel,
        grid=(6 * D_MODEL // tn,),
        in_specs=[pl.BlockSpec((n, D_MODEL), lambda j: (0, 0)),
                  pl.BlockSpec((D_MODEL, tn), lambda j: (0, j)),
                  pl.BlockSpec((1, tn), lambda j: (0, j))],
        out_specs=pl.BlockSpec((n, tn), lambda j: (0, j)),
        out_shape=jax.ShapeDtypeStruct((n, 6 * D_MODEL), F32),
        compiler_params=_params("arbitrary"),
        name="ada_modulation",
    )(c, w_ada, b_ada.reshape(1, -1))


def _in_proj_kernel(x_ref, shift_ref, scale_ref, w_ref,
                    qa_ref, ka_ref, va_ref, kbf_ref, vbf_ref, kmean_ref,
                    qr_ref, kr_ref, vr_ref, gr_ref, ga_ref, gb_ref):
    h = (_layer_norm(x_ref[...]) * (1.0 + scale_ref[0]) + shift_ref[0]).astype(BF16)

    def proj(start, width):
        return jnp.dot(h, w_ref[:, start:start + width], preferred_element_type=F32)

    qa_ref[...] = proj(0, ATT_W)
    ka = proj(ATT_W, ATT_W)
    ka_ref[...] = ka
    kbf_ref[...] = ka.astype(BF16)
    n_blk = ka.shape[0] // MOBA_BLOCK
    for b in range(n_blk):
        kmean_ref[b] = jnp.mean(ka[b * MOBA_BLOCK:(b + 1) * MOBA_BLOCK], axis=0, keepdims=True)
    va = proj(2 * ATT_W, ATT_W)
    va_ref[...] = va
    vbf_ref[...] = va.astype(BF16)
    off = 3 * ATT_W
    qr_ref[...] = proj(off, RET_QK_W)
    kr_ref[...] = proj(off + RET_QK_W, RET_QK_W)
    off += 2 * RET_QK_W
    vr_ref[...] = proj(off, RET_V_W).astype(BF16)
    gr_ref[...] = proj(off + RET_V_W, RET_V_W)
    off += 2 * RET_V_W
    ga_ref[...] = proj(off, D_MODEL)
    gb_ref[...] = proj(off + D_MODEL, D_MODEL)


def in_proj(x, shift, scale, w_in_bf16, tiles_per_group):
    m = x.shape[0]
    tm = ROW_TILE
    r = shift.shape[1]
    row = lambda w: pl.BlockSpec((tm, w), lambda i: (i, 0))
    mod = pl.BlockSpec((1, r, D_MODEL), lambda i: (i // tiles_per_group, 0, 0))
    widths = dict(qa=ATT_W, ka=ATT_W, va=ATT_W, kbf=ATT_W, vbf=ATT_W, qr=RET_QK_W, kr=RET_QK_W,
                  vr=RET_V_W, gr=RET_V_W, ga=D_MODEL, gb=D_MODEL)
    dtypes = dict(kbf=BF16, vbf=BF16, vr=BF16)
    names = ["qa", "ka", "va", "kbf", "vbf", "kmean", "qr", "kr", "vr", "gr", "ga", "gb"]
    out_specs, out_shapes = [], []
    for nm in names:
        if nm == "kmean":
            out_specs.append(pl.BlockSpec((tm // MOBA_BLOCK, 1, ATT_W), lambda i: (i, 0, 0)))
            out_shapes.append(jax.ShapeDtypeStruct((m // MOBA_BLOCK, 1, ATT_W), F32))
        else:
            out_specs.append(row(widths[nm]))
            out_shapes.append(jax.ShapeDtypeStruct((m, widths[nm]), dtypes.get(nm, F32)))
    outs = pl.pallas_call(
        _in_proj_kernel,
        grid=(m // tm,),
        in_specs=[row(D_MODEL), mod, mod, _resident((D_MODEL, D_IN))],
        out_specs=out_specs,
        out_shape=out_shapes,
        compiler_params=_params("arbitrary"),
        name="in_proj",
    )(x, shift, scale, w_in_bf16)
    return dict(zip(names, outs))


def t5_bucket(rel):
    n = jnp.maximum(rel, 0)
    max_exact = N_BUCKETS // 2
    ratio = jnp.log(jnp.maximum(n, max_exact).astype(F32) / max_exact) / math.log(MAX_DISTANCE / max_exact)
    large = jnp.minimum(max_exact + (ratio * (N_BUCKETS - max_exact)).astype(jnp.int32), N_BUCKETS - 1)
    return jnp.where(n < max_exact, n, large)


def _bias_from_buckets(bucket, rel_bias_ref, head):
    out = jnp.zeros(bucket.shape, F32)
    for b in range(N_BUCKETS):
        out = jnp.where(bucket == b, rel_bias_ref[b, head], out)
    return out


def _bias_rows_from_buckets(bucket, relb_rows):
    out = jnp.zeros(bucket.shape, F32)
    for b in range(N_BUCKETS):
        out = jnp.where(bucket == b, relb_rows[:, b:b + 1], out)
    return out


def _top_block_ids(scores, n_eligible, n_pick):
    n_blocks = scores.shape[1]
    blk = lax.broadcasted_iota(jnp.int32, scores.shape, 1)
    sc = jnp.where(blk < n_eligible, scores, NEG_INF)
    picks = []
    for _ in range(n_pick):
        best = jnp.max(sc, axis=1, keepdims=True)
        first = jnp.min(jnp.where(sc == best, blk, n_blocks), axis=1, keepdims=True)
        picks.append(jnp.where(first < n_eligible, first, -1))
        sc = jnp.where(blk == first, NEG_INF, sc)
    return picks


def _picked(picks, block_id):
    keep = picks[0] == block_id
    for p in picks[1:]:
        keep = jnp.logical_or(keep, p == block_id)
    return keep


def _moba_prompt_kernel(relb_ref, q_ref, k_ref, v_ref, kmean_ref, bkt_own_ref, bkt_prev_ref,
                        o_ref, bias_own, bias_prev):
    n_i = pl.program_id(0)
    kb = pl.program_id(1)
    blk_rows = MOBA_BLOCK

    @pl.when(jnp.logical_and(n_i == 0, kb == 0))
    def _():
        for h in range(H_A):
            bias_own[h] = _bias_from_buckets(bkt_own_ref[...], relb_ref, h)
            bias_prev[h] = _bias_from_buckets(bkt_prev_ref[...], relb_ref, h)

    q = q_ref[0]
    kmean = kmean_ref[0]
    row = lax.broadcasted_iota(jnp.int32, (blk_rows, blk_rows), 0)
    col = lax.broadcasted_iota(jnp.int32, (blk_rows, blk_rows), 1)
    causal = row >= col
    own_start = pl.multiple_of(kb * blk_rows, blk_rows)
    prev_blk = jnp.maximum(kb - 1, 0)
    prev_start = pl.multiple_of(prev_blk * blk_rows, blk_rows)
    head_lanes = [slice(h * HD_A, (h + 1) * HD_A) for h in range(H_A)]
    picks = [_top_block_ids(lax.dot_general(q[:, ln], kmean[:, ln], NT_DIMS, precision=HIGHEST,
                                            preferred_element_type=F32), kb, MOBA_TOPK)
             for ln in head_lanes]
    qs = [(q[:, ln] * (HD_A ** -0.5)).astype(BF16) for ln in head_lanes]

    def attend(h, k_start, bias, keep, carry):
        m, l, acc = carry
        kh = k_ref[0, pl.ds(k_start, blk_rows), head_lanes[h]]
        vh = v_ref[0, pl.ds(k_start, blk_rows), head_lanes[h]]
        s = lax.dot_general(qs[h], kh, NT_DIMS, preferred_element_type=F32) + bias
        s = jnp.where(keep, s, NEG_INF)
        m_new = jnp.maximum(m, jnp.max(s, axis=1, keepdims=True))
        alpha = jnp.exp(m - m_new)
        p = jnp.exp(s - m_new)
        l = alpha * l + jnp.sum(p, axis=1, keepdims=True)
        acc = alpha * acc + jnp.dot(p.astype(BF16), vh, preferred_element_type=F32)
        return m_new, l, acc

    carries = []
    for h in range(H_A):
        s0 = lax.dot_general(qs[h], k_ref[0, pl.ds(own_start, blk_rows), head_lanes[h]], NT_DIMS,
                             preferred_element_type=F32) + bias_own[h]
        s0 = jnp.where(causal, s0, NEG_INF)
        m0 = jnp.max(s0, axis=1, keepdims=True)
        p0 = jnp.exp(s0 - m0)
        carry = (m0, jnp.sum(p0, axis=1, keepdims=True),
                 jnp.dot(p0.astype(BF16), v_ref[0, pl.ds(own_start, blk_rows), head_lanes[h]],
                         preferred_element_type=F32))
        carries.append(attend(h, prev_start, bias_prev[h], _picked(picks[h], prev_blk), carry))

    def far_body(j, cs):
        k_start = pl.multiple_of(j * blk_rows, blk_rows)
        return tuple(attend(h, k_start, relb_ref[N_BUCKETS - 1, h], _picked(picks[h], j), cs[h])
                     for h in range(H_A))

    carries = lax.fori_loop(0, jnp.maximum(kb - 1, 0), far_body, tuple(carries))
    outs = [acc / l for (_, l, acc) in carries]
    o_ref[0] = jnp.concatenate(outs, axis=-1).astype(o_ref.dtype)


def moba_prompt(q, kbf, vbf, kmean, rel_bias):
    n, s, _ = q.shape
    nb = s // MOBA_BLOCK
    i = jnp.arange(MOBA_BLOCK)
    rel_own = i[:, None] - i[None, :]
    bkt_own = t5_bucket(rel_own).astype(jnp.int32)
    bkt_prev = t5_bucket(rel_own + MOBA_BLOCK).astype(jnp.int32)
    grid_spec = pltpu.PrefetchScalarGridSpec(
        num_scalar_prefetch=1,
        grid=(n, nb),
        in_specs=[pl.BlockSpec((1, MOBA_BLOCK, ATT_W), lambda a, b, *_: (a, b, 0)),
                  pl.BlockSpec((1, s, ATT_W), lambda a, b, *_: (a, 0, 0)),
                  pl.BlockSpec((1, s, ATT_W), lambda a, b, *_: (a, 0, 0)),
                  pl.BlockSpec((1, nb, ATT_W), lambda a, b, *_: (a, 0, 0)),
                  pl.BlockSpec((MOBA_BLOCK, MOBA_BLOCK), lambda a, b, *_: (0, 0)),
                  pl.BlockSpec((MOBA_BLOCK, MOBA_BLOCK), lambda a, b, *_: (0, 0))],
        out_specs=pl.BlockSpec((1, MOBA_BLOCK, ATT_W), lambda a, b, *_: (a, b, 0)),
        scratch_shapes=[pltpu.VMEM((H_A, MOBA_BLOCK, MOBA_BLOCK), F32),
                        pltpu.VMEM((H_A, MOBA_BLOCK, MOBA_BLOCK), F32)],
    )
    return pl.pallas_call(
        _moba_prompt_kernel,
        grid_spec=grid_spec,
        out_shape=jax.ShapeDtypeStruct((n, s, ATT_W), BF16),
        compiler_params=_params("arbitrary", "arbitrary"),
        name="moba_prompt",
    )(rel_bias, q, kbf, vbf, kmean, bkt_own, bkt_prev)


def _moba_sample_kernel(pt_ref, q_ref, knew_ref, vnew_ref, relb_rows_ref, bkt_last_ref, bkt_own_ref,
                        ck_hbm, cv_hbm, o_ref,
                        kbuf, vbuf, kown, vown, logit, prob, means, bias_last, bias_own, sems):
    seq = pl.program_id(0)
    n_seq = pl.num_programs(0)
    n_pages = kbuf.shape[0]
    n_blocks = n_pages // PAGES_PER_BLOCK
    t_new = q_ref.shape[1]
    n_rows = t_new * H_A

    def page_copy(cache, buf, sem_slot, s, j):
        return pltpu.make_async_copy(cache.at[pt_ref[s, j]], buf.at[j], sems.at[sem_slot])

    def start_pages(cache, buf, sem_slot, s):
        for j in range(n_pages):
            page_copy(cache, buf, sem_slot, s, j).start()

    def wait_pages(cache, buf, sem_slot, s):
        for j in range(n_pages):
            page_copy(cache, buf, sem_slot, s, j).wait()

    @pl.when(seq == 0)
    def _():
        start_pages(ck_hbm, kbuf, 0, 0)
        start_pages(cv_hbm, vbuf, 1, 0)
        kown[...] = jnp.zeros_like(kown)
        vown[...] = jnp.zeros_like(vown)
        means[...] = jnp.zeros_like(means)
        relb_rows = relb_rows_ref[...]
        bias_last[...] = _bias_rows_from_buckets(bkt_last_ref[...], relb_rows)
        bias_own[...] = _bias_rows_from_buckets(bkt_own_ref[...], relb_rows)

    q = q_ref[0]
    head_mask = (lax.broadcasted_iota(jnp.int32, (H_A, ATT_W), 1) // HD_A
                 == lax.broadcasted_iota(jnp.int32, (H_A, ATT_W), 0))
    qbd = jnp.concatenate([jnp.where(head_mask, jnp.broadcast_to(q[t:t + 1], (H_A, ATT_W)), 0.0)
                           for t in range(t_new)], axis=0)
    qs = (qbd * (HD_A ** -0.5)).astype(BF16)
    kown[0:t_new, :] = knew_ref[0]
    vown[0:t_new, :] = vnew_ref[0]

    wait_pages(ck_hbm, kbuf, 0, seq)

    blk_lane = lax.broadcasted_iota(jnp.int32, means.shape, 1)

    def k_block(b, carry):
        tot = jnp.zeros((ATT_W, PAGE_SIZE), F32)
        for pg in range(PAGES_PER_BLOCK):
            j = b * PAGES_PER_BLOCK + pg
            kp = kbuf[j]
            tot = tot + kp
            logit[j] = jnp.dot(qs, kp.astype(BF16), preferred_element_type=F32)
        mean_col = jnp.sum(tot, axis=1, keepdims=True) * (1.0 / MOBA_BLOCK)
        means[...] = jnp.where(blk_lane == b, mean_col, means[...])
        return carry

    lax.fori_loop(0, n_blocks, k_block, 0)

    @pl.when(seq + 1 < n_seq)
    def _():
        start_pages(ck_hbm, kbuf, 0, seq + 1)

    scores = jnp.dot(qbd, means[:, 0:n_blocks], precision=HIGHEST, preferred_element_type=F32)
    picks = _top_block_ids(scores, n_blocks, min(MOBA_TOPK, n_blocks))

    far_bias = relb_rows_ref[:, N_BUCKETS - 1:N_BUCKETS]
    own_ok = bkt_own_ref[...] >= 0
    s_own = lax.dot_general(qs, kown[...].astype(BF16), NT_DIMS, preferred_element_type=F32)
    s_own = jnp.where(own_ok, s_own + bias_own[...], NEG_INF)
    mx = s_own
    for j in range(n_pages):
        b, pg = divmod(j, PAGES_PER_BLOCK)
        if b == n_blocks - 1:
            bias = bias_last[:, pg * PAGE_SIZE:(pg + 1) * PAGE_SIZE]
        else:
            bias = far_bias
        s = jnp.where(_picked(picks, b), logit[j] + bias, NEG_INF)
        logit[j] = s
        mx = jnp.maximum(mx, s)
    m = jnp.max(mx, axis=1, keepdims=True)
    p_own = jnp.exp(s_own - m)
    lsum = p_own
    for j in range(n_pages):
        p = jnp.exp(logit[j] - m)
        prob[j] = p.astype(BF16)
        lsum = lsum + p
    l = jnp.sum(lsum, axis=1, keepdims=True)

    wait_pages(cv_hbm, vbuf, 1, seq)

    def v_block(b, acc):
        for pg in range(PAGES_PER_BLOCK):
            j = b * PAGES_PER_BLOCK + pg
            acc = acc + lax.dot_general(prob[j], vbuf[j].astype(BF16), NT_DIMS,
                                        preferred_element_type=F32)
        return acc

    acc = lax.fori_loop(0, n_blocks, v_block, jnp.zeros((n_rows, ATT_W), F32))

    @pl.when(seq + 1 < n_seq)
    def _():
        start_pages(cv_hbm, vbuf, 1, seq + 1)

    acc = acc + jnp.dot(p_own.astype(BF16), vown[...].astype(BF16), preferred_element_type=F32)
    acc = acc / l
    for t in range(t_new):
        rows = acc[t * H_A:(t + 1) * H_A]
        o_ref[0, t:t + 1, :] = jnp.sum(jnp.where(head_mask, rows, 0.0), axis=0, keepdims=True)


def moba_sample(q, k_new, v_new, cache_k, cache_v, page_table, rel_bias):
    n, t, _ = q.shape
    n_pages = page_table.shape[1]
    assert n_pages % PAGES_PER_BLOCK == 0 and PAGES_PER_BLOCK <= n_pages <= PAGE_SIZE * PAGES_PER_BLOCK
    n_rows = t * H_A
    t_of_row = jnp.arange(n_rows) // H_A
    relb_rows = jnp.tile(rel_bias.T, (t, 1))
    bkt_last = t5_bucket(MOBA_BLOCK + t_of_row[:, None] - jnp.arange(MOBA_BLOCK)[None, :]).astype(jnp.int32)
    cols = jnp.arange(PAGE_SIZE)[None, :]
    rel_own = t_of_row[:, None] - cols
    bkt_own = jnp.where((rel_own >= 0) & (cols < t), t5_bucket(rel_own), -1).astype(jnp.int32)
    seq_spec = pl.BlockSpec((1, t, ATT_W), lambda s, *_: (s, 0, 0))
    const = lambda shape: pl.BlockSpec(shape, lambda s, *_: (0,) * len(shape))
    grid_spec = pltpu.PrefetchScalarGridSpec(
        num_scalar_prefetch=1,
        grid=(n,),
        in_specs=[seq_spec, seq_spec, seq_spec,
                  const((n_rows, N_BUCKETS)), const((n_rows, MOBA_BLOCK)), const((n_rows, PAGE_SIZE)),
                  pl.BlockSpec(memory_space=pl.ANY), pl.BlockSpec(memory_space=pl.ANY)],
        out_specs=seq_spec,
        scratch_shapes=[pltpu.VMEM((n_pages, ATT_W, PAGE_SIZE), F32),
                        pltpu.VMEM((n_pages, ATT_W, PAGE_SIZE), F32),
                        pltpu.VMEM((PAGE_SIZE, ATT_W), F32),
                        pltpu.VMEM((PAGE_SIZE, ATT_W), F32),
                        pltpu.VMEM((n_pages, n_rows, PAGE_SIZE), F32),
                        pltpu.VMEM((n_pages, n_rows, PAGE_SIZE), BF16),
                        pltpu.VMEM((ATT_W, PAGE_SIZE), F32),
                        pltpu.VMEM((n_rows, MOBA_BLOCK), F32),
                        pltpu.VMEM((n_rows, PAGE_SIZE), F32),
                        pltpu.SemaphoreType.DMA((2,))],
    )
    return pl.pallas_call(
        _moba_sample_kernel,
        grid_spec=grid_spec,
        out_shape=jax.ShapeDtypeStruct((n, t, ATT_W), F32),
        compiler_params=_params("arbitrary"),
        name="moba_sample",
    )(page_table, q, k_new, v_new, relb_rows, bkt_last, bkt_own, cache_k, cache_v)


def _swap_pairs(x):
    lane = lax.broadcasted_iota(jnp.int32, x.shape, 1)
    width = x.shape[1]
    return jnp.where(lane % 2 == 0, pltpu.roll(x, width - 1, 1), pltpu.roll(x, 1, 1))


def _retention_kernel(q_ref, k_ref, v_ref, g_ref, cos_ref, sin_ref, decay_ref, cross_ref, kw_ref,
                      sdec_ref, ng_ref, nb_ref, st_in_ref, y_ref, st_out_ref, st):
    c = pl.program_id(1)

    @pl.when(c == 0)
    def _():
        st[...] = st_in_ref[0]

    cos = cos_ref[...]
    sin = sin_ref[...]
    q = q_ref[0]
    k = k_ref[0]
    v = v_ref[0]
    g = g_ref[0]
    for h in range(H_R):
        lanes = slice(h * DK_R, (h + 1) * DK_R)
        qh = q[:, lanes]
        kh = k[:, lanes]
        qh = (qh * cos + _swap_pairs(qh) * sin).astype(BF16)
        kh = (kh * cos + _swap_pairs(kh) * sin) * (DK_R ** -0.5)
        vh = v[:, lanes]
        state = st[h]
        scores = lax.dot_general(qh, kh.astype(BF16), NT_DIMS, preferred_element_type=F32) * decay_ref[h]
        inner = jnp.dot(scores.astype(BF16), vh, preferred_element_type=F32)
        cross = jnp.dot(qh, state.astype(BF16), preferred_element_type=F32) * cross_ref[:, lanes]
        kd = (kh * kw_ref[:, lanes]).astype(BF16)
        st[h] = sdec_ref[h] * state + lax.dot_general(kd, vh, TN_DIMS, preferred_element_type=F32)
        o = inner + cross
        o = _layer_norm(o) * ng_ref[:, lanes] + nb_ref[:, lanes]
        y_ref[0, :, lanes] = (_silu(g[:, lanes]) * o).astype(y_ref.dtype)

    @pl.when(c == pl.num_programs(1) - 1)
    def _():
        st_out_ref[0] = st[...]


def _retention_tables(pos, n_valid, chunk):
    theta = 1.0 / (ROPE_BASE ** jnp.linspace(0.0, 1.0, DK_R // 2, dtype=F32))
    ang = pos.astype(F32)[:, None] * theta[None, :]
    cos = jnp.repeat(jnp.cos(ang), 2, axis=1)
    sin = jnp.repeat(jnp.sin(ang), 2, axis=1) * jnp.tile(jnp.array([-1.0, 1.0], F32), DK_R // 2)[None, :]
    log_g = jnp.log1p(-jnp.exp2(-5.0 - jnp.arange(H_R, dtype=F32)))
    i = jnp.arange(chunk, dtype=F32)
    valid = i < n_valid
    diff = i[:, None] - i[None, :]
    decay = jnp.where(diff >= 0, jnp.exp(jnp.maximum(diff, 0.0)[None] * log_g[:, None, None]), 0.0)
    decay = jnp.where(valid[None, :, None] & valid[None, None, :], decay, 0.0)
    cross = jnp.where(valid[:, None], jnp.exp((i + 1.0)[:, None] * log_g[None, :]), 0.0)
    kw = jnp.where(valid[:, None], jnp.exp((n_valid - 1.0 - i)[:, None] * log_g[None, :]), 0.0)
    sdec = jnp.exp(n_valid * log_g)
    return dict(cos=cos, sin=sin, decay=decay,
                cross=jnp.repeat(cross, DV_R, axis=1), kw=jnp.repeat(kw, DK_R, axis=1),
                sdec=jnp.broadcast_to(sdec[:, None, None], (H_R, 1, DV_R)))


def retention(q, k, v, g, state0, tables, norm_g, norm_b):
    n, s, _ = q.shape
    chunk = tables["decay"].shape[1]
    n_chunks = s // chunk
    seq = pl.BlockSpec((1, chunk, RET_QK_W), lambda a, c: (a, c, 0))
    per_chunk = pl.BlockSpec((chunk, DK_R), lambda a, c: (c, 0))
    per_row = pl.BlockSpec((chunk, RET_QK_W), lambda a, c: (0, 0))
    st_spec = pl.BlockSpec((1, H_R, DK_R, DV_R), lambda a, c: (a, 0, 0, 0))
    vec = pl.BlockSpec((1, RET_V_W), lambda a, c: (0, 0))
    return pl.pallas_call(
        _retention_kernel,
        grid=(n, n_chunks),
        in_specs=[seq, seq, seq, seq, per_chunk, per_chunk,
                  pl.BlockSpec((H_R, chunk, chunk), lambda a, c: (0, 0, 0)),
                  per_row, per_row,
                  pl.BlockSpec((H_R, 1, DV_R), lambda a, c: (0, 0, 0)),
                  vec, vec, st_spec],
        out_specs=[seq, st_spec],
        out_shape=[jax.ShapeDtypeStruct((n, s, RET_V_W), BF16),
                   jax.ShapeDtypeStruct((n, H_R, DK_R, DV_R), F32)],
        scratch_shapes=[pltpu.VMEM((H_R, DK_R, DV_R), F32)],
        compiler_params=_params("arbitrary", "arbitrary"),
        name="retention",
    )(q, k, v, g, tables["cos"], tables["sin"], tables["decay"], tables["cross"], tables["kw"],
      tables["sdec"], norm_g.reshape(1, -1), norm_b.reshape(1, -1), state0)


def _mix_kernel(att_ref, ret_ref, ga_ref, gb_ref, x_ref, gate1_ref, shift2_ref, scale2_ref,
                wa_ref, wb_ref, wo_ref, g1_ref, b1_ref, wr_ref, x1_ref, h2_ref, logit_ref):
    ya = jnp.dot(att_ref[...], wa_ref[...], preferred_element_type=F32)
    yb = jnp.dot(ret_ref[...], wb_ref[...], preferred_element_type=F32)
    mixed_in = jax.nn.sigmoid(ga_ref[...]) * ya + jax.nn.sigmoid(gb_ref[...]) * yb
    mixed = jnp.dot(mixed_in.astype(BF16), wo_ref[...], preferred_element_type=F32)
    x1 = _layer_norm(ALPHA * x_ref[...] + gate1_ref[0] * mixed) * g1_ref[...] + b1_ref[...]
    x1_ref[...] = x1
    h2 = _layer_norm(x1) * (1.0 + scale2_ref[0]) + shift2_ref[0]
    h2_ref[...] = h2
    logit_ref[...] = jnp.dot(h2, wr_ref[...], precision=HIGHEST, preferred_element_type=F32)


def mix_out(att, ret, ga, gb, x, gate1, shift2, scale2, wa, wb, wo, ln_g, ln_b, w_router, tiles_per_group):
    m = x.shape[0]
    tm = ROW_TILE
    r = gate1.shape[1]
    row = lambda w: pl.BlockSpec((tm, w), lambda i: (i, 0))
    mod = pl.BlockSpec((1, r, D_MODEL), lambda i: (i // tiles_per_group, 0, 0))
    return pl.pallas_call(
        _mix_kernel,
        grid=(m // tm,),
        in_specs=[row(ATT_W), row(RET_V_W), row(D_MODEL), row(D_MODEL), row(D_MODEL), mod, mod, mod,
                  _resident((ATT_W, D_MODEL)), _resident((RET_V_W, D_MODEL)), _resident((D_MODEL, D_MODEL)),
                  _resident((1, D_MODEL)), _resident((1, D_MODEL)), _resident((D_MODEL, N_EXPERTS))],
        out_specs=[row(D_MODEL), row(D_MODEL), row(N_EXPERTS)],
        out_shape=[jax.ShapeDtypeStruct((m, D_MODEL), F32), jax.ShapeDtypeStruct((m, D_MODEL), F32),
                   jax.ShapeDtypeStruct((m, N_EXPERTS), F32)],
        compiler_params=_params("arbitrary"),
        name="mix_out",
    )(att, ret, ga, gb, x, gate1, shift2, scale2, wa, wb, wo, ln_g.reshape(1, -1), ln_b.reshape(1, -1),
      w_router)


def _first_max(x, ids, sentinel):
    best = jnp.max(x, axis=0, keepdims=True)
    first = jnp.min(jnp.where(x == best, ids, sentinel), axis=0, keepdims=True)
    return best, first


def _route_kernel(logit_ref, bias_ref, idx_ref, gate_ref):
    s = jax.nn.sigmoid(logit_ref[...].T)
    sb = s + bias_ref[...]
    tm = s.shape[1]
    gid = lax.broadcasted_iota(jnp.int32, (GROUP_SIZE, tm), 0)
    grp_scores = []
    for gi in range(N_GROUPS):
        grp = sb[gi * GROUP_SIZE:(gi + 1) * GROUP_SIZE]
        top1, first = _first_max(grp, gid, GROUP_SIZE)
        top2 = jnp.max(jnp.where(gid == first, NEG_INF, grp), axis=0, keepdims=True)
        grp_scores.append(top1 + top2)
    gs = jnp.concatenate(grp_scores, axis=0)
    gids = lax.broadcasted_iota(jnp.int32, (N_GROUPS, tm), 0)
    gmask = jnp.zeros((N_GROUPS, tm), jnp.bool_)
    for _ in range(TOPK_GROUPS):
        _, first = _first_max(gs, gids, N_GROUPS)
        hit = gids == first
        gmask = jnp.logical_or(gmask, hit)
        gs = jnp.where(hit, NEG_INF, gs)
    masked = jnp.concatenate(
        [jnp.where(gmask[gi:gi + 1], sb[gi * GROUP_SIZE:(gi + 1) * GROUP_SIZE], NEG_INF)
         for gi in range(N_GROUPS)], axis=0)
    eids = lax.broadcasted_iota(jnp.int32, (N_EXPERTS, tm), 0)
    picks, weights = [], []
    for _ in range(TOP_K):
        _, first = _first_max(masked, eids, N_EXPERTS)
        hit = eids == first
        picks.append(first)
        weights.append(jnp.sum(jnp.where(hit, s, 0.0), axis=0, keepdims=True))
        masked = jnp.where(hit, NEG_INF, masked)
    w = jnp.concatenate(weights, axis=0)
    idx_ref[...] = jnp.concatenate(picks, axis=0)
    gate_ref[...] = w / jnp.sum(w, axis=0, keepdims=True) * ROUTED_SCALE


def route(logits, router_bias):
    m = logits.shape[0]
    tm = ROW_TILE
    return pl.pallas_call(
        _route_kernel,
        grid=(m // tm,),
        in_specs=[pl.BlockSpec((tm, N_EXPERTS), lambda i: (i, 0)),
                  pl.BlockSpec((N_EXPERTS, 1), lambda i: (0, 0))],
        out_specs=[pl.BlockSpec((TOP_K, tm), lambda i: (0, i)), pl.BlockSpec((TOP_K, tm), lambda i: (0, i))],
        out_shape=[jax.ShapeDtypeStruct((TOP_K, m), jnp.int32), jax.ShapeDtypeStruct((TOP_K, m), F32)],
        compiler_params=_params("arbitrary"),
        name="route",
    )(logits, router_bias.reshape(-1, 1))


def _expert_kernel(blk_e_ref, n_used_ref, tok_ref, tok_next_ref, wg_ref, wu_ref, wd_ref, h_hbm,
                   y_ref, xbuf, sems):
    i = pl.program_id(0)
    n_used = n_used_ref[0]
    tm = xbuf.shape[1]
    slot = i % 2

    def row_copy(tok_smem, r, to_slot):
        return pltpu.make_async_copy(h_hbm.at[pl.ds(tok_smem[0, 0, r], 1)],
                                     xbuf.at[to_slot, pl.ds(r, 1)], sems.at[to_slot])

    def start_rows(tok_smem, to_slot):
        for r in range(tm):
            row_copy(tok_smem, r, to_slot).start()

    def wait_rows(to_slot):
        pltpu.make_async_copy(h_hbm.at[pl.ds(0, tm)], xbuf.at[to_slot], sems.at[to_slot]).wait()

    @pl.when(jnp.logical_and(i == 0, n_used > 0))
    def _():
        start_rows(tok_ref, 0)

    @pl.when(i + 1 < n_used)
    def _():
        start_rows(tok_next_ref, 1 - slot)

    @pl.when(i < n_used)
    def _():
        wait_rows(slot)
        x = xbuf[slot].astype(BF16)
        a = (_silu(jnp.dot(x, wg_ref[0].astype(BF16), preferred_element_type=F32))
             * jnp.dot(x, wu_ref[0].astype(BF16), preferred_element_type=F32))
        y_ref[...] = jnp.dot(a.astype(BF16), wd_ref[0].astype(BF16), preferred_element_type=F32)

    @pl.when(i >= n_used)
    def _():
        y_ref[...] = jnp.zeros_like(y_ref)


def routed_experts(h, buf_tok, blk_e, n_used, w_gate, w_up, w_down):
    n_blocks, _, tm = buf_tok.shape
    last = n_blocks - 1
    grid_spec = pltpu.PrefetchScalarGridSpec(
        num_scalar_prefetch=2,
        grid=(n_blocks,),
        in_specs=[pl.BlockSpec((1, 1, tm), lambda i, be, nu: (i, 0, 0), memory_space=pltpu.SMEM),
                  pl.BlockSpec((1, 1, tm), lambda i, be, nu: (jnp.minimum(i + 1, last), 0, 0),
                               memory_space=pltpu.SMEM),
                  pl.BlockSpec((1, D_MODEL, D_EXPERT), lambda i, be, nu: (be[i], 0, 0)),
                  pl.BlockSpec((1, D_MODEL, D_EXPERT), lambda i, be, nu: (be[i], 0, 0)),
                  pl.BlockSpec((1, D_EXPERT, D_MODEL), lambda i, be, nu: (be[i], 0, 0)),
                  pl.BlockSpec(memory_space=pl.ANY)],
        out_specs=pl.BlockSpec((tm, D_MODEL), lambda i, be, nu: (i, 0)),
        scratch_shapes=[pltpu.VMEM((2, tm, D_MODEL), F32), pltpu.SemaphoreType.DMA((2,))],
    )
    return pl.pallas_call(
        _expert_kernel,
        grid_spec=grid_spec,
        out_shape=jax.ShapeDtypeStruct((n_blocks * tm, D_MODEL), F32),
        compiler_params=_params("arbitrary"),
        name="routed_experts",
    )(blk_e, n_used, buf_tok, buf_tok, w_gate, w_up, w_down, h)


def _final_kernel(dest_ref, dest_next_ref, x1_ref, h2_ref, gate_ref, gate2_ref, wsg_ref, wsu_ref, wsd_ref,
                  g2_ref, b2_ref, ys_hbm, y_ref, rbuf, sems):
    i = pl.program_id(0)
    n_steps = pl.num_programs(0)
    tm = x1_ref.shape[0]
    n_gather = TOP_K * tm
    slot = i % 2

    def row_copy(dest_smem, r, to_slot):
        return pltpu.make_async_copy(ys_hbm.at[pl.ds(dest_smem[0, 0, r], 1)],
                                     rbuf.at[to_slot, pl.ds(r, 1)], sems.at[to_slot])

    def start_rows(dest_smem, to_slot):
        def body(r, c):
            row_copy(dest_smem, r, to_slot).start()
            return c
        lax.fori_loop(0, n_gather, body, 0, unroll=8)

    def wait_rows(to_slot):
        pltpu.make_async_copy(ys_hbm.at[pl.ds(0, n_gather)], rbuf.at[to_slot], sems.at[to_slot]).wait()

    @pl.when(i == 0)
    def _():
        start_rows(dest_ref, 0)

    @pl.when(i + 1 < n_steps)
    def _():
        start_rows(dest_next_ref, 1 - slot)

    h = h2_ref[...].astype(BF16)
    a = (_silu(jnp.dot(h, wsg_ref[...], preferred_element_type=F32))
         * jnp.dot(h, wsu_ref[...], preferred_element_type=F32))
    f = jnp.dot(a.astype(BF16), wsd_ref[...], preferred_element_type=F32)
    wait_rows(slot)
    gate = gate_ref[...]
    routed = jnp.zeros((tm, D_MODEL), F32)
    for k in range(TOP_K):
        routed = routed + rbuf[slot, k * tm:(k + 1) * tm, :] * gate[:, k:k + 1]
    f = routed + f
    y_ref[...] = _layer_norm(ALPHA * x1_ref[...] + gate2_ref[...] * f) * g2_ref[...] + b2_ref[...]


def final_out(x1, h2, gate, gate2_rows, dest, ys, wsg, wsu, wsd, ln_g, ln_b):
    m = x1.shape[0]
    tm = COMBINE_TILE
    n_steps = m // tm
    row = lambda w: pl.BlockSpec((tm, w), lambda i: (i, 0))
    dest_spec = lambda f: pl.BlockSpec((1, 1, TOP_K * tm), lambda i: (f(i), 0, 0), memory_space=pltpu.SMEM)
    return pl.pallas_call(
        _final_kernel,
        grid=(n_steps,),
        in_specs=[dest_spec(lambda i: i), dest_spec(lambda i: jnp.minimum(i + 1, n_steps - 1)),
                  row(D_MODEL), row(D_MODEL), row(TOP_K), row(D_MODEL),
                  _resident((D_MODEL, D_SHARED)), _resident((D_MODEL, D_SHARED)), _resident((D_SHARED, D_MODEL)),
                  _resident((1, D_MODEL)), _resident((1, D_MODEL)),
                  pl.BlockSpec(memory_space=pl.ANY)],
        out_specs=row(D_MODEL),
        out_shape=jax.ShapeDtypeStruct((m, D_MODEL), F32),
        scratch_shapes=[pltpu.VMEM((2, TOP_K * tm, D_MODEL), F32), pltpu.SemaphoreType.DMA((2,))],
        compiler_params=_params("arbitrary"),
        name="final_out",
    )(dest, dest, x1, h2, gate, gate2_rows, wsg, wsu, wsd, ln_g.reshape(1, -1), ln_b.reshape(1, -1), ys)


def _dispatch(idx_t):
    m = idx_t.shape[1]
    bm = MOE_ROW_TILE
    n_pairs = m * TOP_K
    experts = jnp.arange(N_EXPERTS, dtype=jnp.int32)
    pair_ids = jnp.arange(n_pairs, dtype=jnp.int32)
    flat_e = idx_t.T.reshape(-1)
    se, order = lax.sort((flat_e, pair_ids), num_keys=1, is_stable=True)
    counts = jnp.sum((flat_e[:, None] == experts[None, :]).astype(jnp.int32), axis=0)
    padded = (counts + bm - 1) // bm * bm
    pad_end = jnp.cumsum(padded)
    pad_start = pad_end - padded
    start = jnp.cumsum(counts) - counts
    shift = (pad_start - start).astype(jnp.int32)
    dest_sorted = pair_ids + jnp.sum(jnp.where(se[:, None] == experts[None, :], shift[None, :], 0), axis=1)
    n_blocks = -(-(n_pairs + N_EXPERTS * (bm - 1)) // bm)
    n_rows = n_blocks * bm
    buf_tok = jnp.zeros((n_rows,), jnp.int32).at[dest_sorted].set(
        order // TOP_K, indices_are_sorted=True, unique_indices=True)
    _, dest = lax.sort((order, dest_sorted), num_keys=1)
    blk_e = jnp.minimum(jnp.searchsorted(pad_end, jnp.arange(n_blocks) * bm, side='right'),
                        N_EXPERTS - 1).astype(jnp.int32)
    n_used = (pad_end[-1] // bm).astype(jnp.int32).reshape(1)
    tc = COMBINE_TILE
    dest_tiles = dest.reshape(m // tc, tc, TOP_K).transpose(0, 2, 1).reshape(m // tc, 1, TOP_K * tc)
    return buf_tok.reshape(n_blocks, 1, bm), blk_e, n_used, dest_tiles


def _expand_rows(v, reps):
    return jnp.repeat(v, reps, axis=0).reshape(-1, ROW_TILE, v.shape[-1])


def kernel(x_prompt, x_sample, cache_k, cache_v, state_ret, page_table, c_prompt, c_sample, w_ada, b_ada, w_in, rel_bias, ret_norm_g, ret_norm_b, w_branch_a, w_branch_b, w_out, ln1_g, ln1_b, w_router, router_bias, w_gate, w_up, w_down, w_shared_gate, w_shared_up, w_shared_down, ln2_g, ln2_b):
    assert DEPTH == 1
    l = 0
    nb, s, d = x_prompt.shape
    ns, t, _ = x_sample.shape
    past_len = page_table.shape[1] * PAGE_SIZE
    mp, ms = nb * s, ns * t

    c_all = jnp.concatenate([c_prompt, c_sample], axis=0)
    n_c = c_all.shape[0]
    c_all = jnp.pad(c_all, ((0, -n_c % 8), (0, 0)))
    mod = ada_modulation(c_all, w_ada[l], b_ada[l])
    mods_p = [mod[:nb, i * d:(i + 1) * d].reshape(nb, 1, d) for i in range(6)]
    mods_s = [_expand_rows(mod[nb:nb + ns, i * d:(i + 1) * d], t) for i in range(6)]
    tiles_p = s // ROW_TILE

    w_in_bf = w_in[l].astype(BF16)
    pp = in_proj(x_prompt.reshape(mp, d), mods_p[0], mods_p[1], w_in_bf, tiles_p)
    ps = in_proj(x_sample.reshape(ms, d), mods_s[0], mods_s[1], w_in_bf, 1)

    att_p = moba_prompt(pp["qa"].reshape(nb, s, ATT_W), pp["kbf"].reshape(nb, s, ATT_W),
                        pp["vbf"].reshape(nb, s, ATT_W), pp["kmean"].reshape(nb, s // MOBA_BLOCK, ATT_W),
                        rel_bias)
    tab_p = _retention_tables(jnp.arange(s), float(RET_CHUNK), RET_CHUNK)
    ret_p, st_p = retention(pp["qr"].reshape(nb, s, -1), pp["kr"].reshape(nb, s, -1),
                            pp["vr"].reshape(nb, s, -1), pp["gr"].reshape(nb, s, -1),
                            jnp.zeros((nb, H_R, DK_R, DV_R), F32), tab_p, ret_norm_g[l], ret_norm_b[l])

    n_pool = cache_k.shape[1]
    page_major = lambda c: jnp.transpose(c[l], (0, 2, 3, 1)).reshape(n_pool, ATT_W, PAGE_SIZE)
    att_s = moba_sample(ps["qa"].reshape(ns, t, ATT_W), ps["ka"].reshape(ns, t, ATT_W),
                        ps["va"].reshape(ns, t, ATT_W), page_major(cache_k), page_major(cache_v),
                        page_table, rel_bias)
    pad_t = lambda a: jnp.pad(a.reshape(ns, t, -1), ((0, 0), (0, RET_CHUNK - t), (0, 0)))
    pos_s = past_len + jnp.arange(RET_CHUNK)
    tab_s = _retention_tables(pos_s, float(t), RET_CHUNK)
    ret_s, st_s = retention(pad_t(ps["qr"]), pad_t(ps["kr"]), pad_t(ps["vr"]), pad_t(ps["gr"]),
                            state_ret[l], tab_s, ret_norm_g[l], ret_norm_b[l])
    ret_s = ret_s[:, :t].reshape(ms, RET_V_W)

    wa, wb, wo = w_branch_a[l].astype(BF16), w_branch_b[l].astype(BF16), w_out[l].astype(BF16)
    x1_p, h2_p, lg_p = mix_out(att_p.reshape(mp, ATT_W), ret_p.reshape(mp, RET_V_W), pp["ga"], pp["gb"],
                               x_prompt.reshape(mp, d), mods_p[2], mods_p[3], mods_p[4], wa, wb, wo,
                               ln1_g[l], ln1_b[l], w_router[l], tiles_p)
    x1_s, h2_s, lg_s = mix_out(att_s.reshape(ms, ATT_W).astype(BF16), ret_s, ps["ga"], ps["gb"],
                               x_sample.reshape(ms, d), mods_s[2], mods_s[3], mods_s[4], wa, wb, wo,
                               ln1_g[l], ln1_b[l], w_router[l], 1)

    x1 = jnp.concatenate([x1_p, x1_s], axis=0)
    h2 = jnp.concatenate([h2_p, h2_s], axis=0)
    idx_t, gate_t = route(jnp.concatenate([lg_p, lg_s], axis=0), router_bias[l])
    buf_tok, blk_e, n_used, dest_tiles = _dispatch(idx_t)
    ys = routed_experts(h2, buf_tok, blk_e, n_used, w_gate[l], w_up[l], w_down[l])
    gate2_rows = jnp.concatenate([jnp.broadcast_to(mods_p[5], (nb, s, d)).reshape(mp, d),
                                  mods_s[5].reshape(ms, d)], axis=0)
    wsg, wsu, wsd = (w_shared_gate[l].astype(BF16), w_shared_up[l].astype(BF16),
                     w_shared_down[l].astype(BF16))
    y = final_out(x1, h2, gate_t.T, gate2_rows, dest_tiles, ys, wsg, wsu, wsd, ln2_g[l], ln2_b[l])

    return (y[:mp].reshape(nb, s, d), y[mp:].reshape(ns, t, d),
            pp["ka"].reshape(1, nb, s, H_A, HD_A), pp["va"].reshape(1, nb, s, H_A, HD_A), st_p[None],
            ps["ka"].reshape(1, ns, t, H_A, HD_A), ps["va"].reshape(1, ns, t, H_A, HD_A), st_s[None])
```

```python
import math

import jax
import jax.numpy as jnp
from jax import lax
from jax.experimental import pallas as pl
from jax.experimental.pallas import tpu as pltpu

D_MODEL = 1024
DEPTH = 1
PAGE_SIZE = 128
H_A = 8
HD_A = 64
ATT_W = H_A * HD_A
MOBA_BLOCK = 256
MOBA_TOPK = 3
PAGES_PER_BLOCK = MOBA_BLOCK // PAGE_SIZE
N_BUCKETS = 32
MAX_DISTANCE = 128
H_R = 4
DK_R = 128
DV_R = 128
RET_QK_W = H_R * DK_R
RET_V_W = H_R * DV_R
RET_CHUNK = 128
ROPE_BASE = 10000.0
N_EXPERTS = 256
TOP_K = 8
N_GROUPS = 8
TOPK_GROUPS = 4
GROUP_SIZE = N_EXPERTS // N_GROUPS
D_EXPERT = 256
D_SHARED = 256
ROUTED_SCALE = 2.5
ALPHA = (2.0 * DEPTH) ** 0.25
LN_EPS = 1e-5
D_IN = 3 * ATT_W + 2 * RET_QK_W + 2 * RET_V_W + 2 * D_MODEL

F32 = jnp.float32
BF16 = jnp.bfloat16
NEG_INF = float("-inf")
HIGHEST = lax.Precision.HIGHEST
NT_DIMS = (((1,), (1,)), ((), ()))
TN_DIMS = (((0,), (0,)), ((), ()))

VMEM_LIMIT_BYTES = 56 * 1024 * 1024
ROW_TILE = 512
MOE_ROW_TILE = 256
COMBINE_TILE = 256


def _params(*semantics):
    return pltpu.CompilerParams(dimension_semantics=semantics, vmem_limit_bytes=VMEM_LIMIT_BYTES)


def _layer_norm(x):
    mu = jnp.mean(x, axis=-1, keepdims=True)
    xc = x - mu
    var = jnp.mean(xc * xc, axis=-1, keepdims=True)
    return xc * lax.rsqrt(var + LN_EPS)


def _silu(x):
    return x * jax.nn.sigmoid(x)


def _pack_bf16_halves(x):
    bits = lax.bitcast_convert_type(x.astype(F32), jnp.uint32)
    w = x.shape[1] // 2
    return bits[:, w:] | (bits[:, :w] >> 16)


def _unpack_bf16_halves(words):
    lo = lax.bitcast_convert_type(words << 16, F32)
    hi = lax.bitcast_convert_type(words & jnp.uint32(0xFFFF0000), F32)
    return jnp.concatenate([lo, hi], axis=1).astype(BF16)


def _resident(shape):
    zeros = (0,) * len(shape)
    return pl.BlockSpec(shape, lambda *_: zeros, pipeline_mode=pl.Buffered(1))


def _ada_kernel(c_ref, w_ref, b_ref, o_ref):
    a = _silu(c_ref[...])
    o_ref[...] = jnp.dot(a, w_ref[...], precision=HIGHEST, preferred_element_type=F32) + b_ref[...]


def ada_modulation(c, w_ada, b_ada):
    n = c.shape[0]
    tn = 1536
    return pl.pallas_call(
        _ada_kernel,
        grid=(6 * D_MODEL // tn,),
        in_specs=[pl.BlockSpec((n, D_MODEL), lambda j: (0, 0)),
                  pl.BlockSpec((D_MODEL, tn), lambda j: (0, j)),
                  pl.BlockSpec((1, tn), lambda j: (0, j))],
        out_specs=pl.BlockSpec((n, tn), lambda j: (0, j)),
        out_shape=jax.ShapeDtypeStruct((n, 6 * D_MODEL), F32),
        compiler_params=_params("arbitrary"),
        name="ada_modulation",
    )(c, w_ada, b_ada.reshape(1, -1))


def _in_proj_kernel(x_ref, shift_ref, scale_ref, w_ref,
                    qa_ref, ka_ref, va_ref, kbf_ref, vbf_ref, kmean_ref,
                    qr_ref, kr_ref, vr_ref, gr_ref, ga_ref, gb_ref):
    h = (_layer_norm(x_ref[...]) * (1.0 + scale_ref[0]) + shift_ref[0]).astype(BF16)

    def proj(start, width):
        return jnp.dot(h, w_ref[:, start:start + width], preferred_element_type=F32)

    qa_ref[...] = proj(0, ATT_W)
    ka = proj(ATT_W, ATT_W)
    ka_ref[...] = ka
    kbf_ref[...] = ka.astype(BF16)
    n_blk = ka.shape[0] // MOBA_BLOCK
    for b in range(n_blk):
        kmean_ref[b] = jnp.mean(ka[b * MOBA_BLOCK:(b + 1) * MOBA_BLOCK], axis=0, keepdims=True)
    va = proj(2 * ATT_W, ATT_W)
    va_ref[...] = va
    vbf_ref[...] = va.astype(BF16)
    off = 3 * ATT_W
    qr_ref[...] = proj(off, RET_QK_W)
    kr_ref[...] = proj(off + RET_QK_W, RET_QK_W)
    off += 2 * RET_QK_W
    vr_ref[...] = proj(off, RET_V_W).astype(BF16)
    gr_ref[...] = proj(off + RET_V_W, RET_V_W)
    off += 2 * RET_V_W
    ga_ref[...] = proj(off, D_MODEL)
    gb_ref[...] = proj(off + D_MODEL, D_MODEL)


def in_proj(x, shift, scale, w_in_bf16, tiles_per_group):
    m = x.shape[0]
    tm = ROW_TILE
    r = shift.shape[1]
    row = lambda w: pl.BlockSpec((tm, w), lambda i: (i, 0))
    mod = pl.BlockSpec((1, r, D_MODEL), lambda i: (i // tiles_per_group, 0, 0))
    widths = dict(qa=ATT_W, ka=ATT_W, va=ATT_W, kbf=ATT_W, vbf=ATT_W, qr=RET_QK_W, kr=RET_QK_W,
                  vr=RET_V_W, gr=RET_V_W, ga=D_MODEL, gb=D_MODEL)
    dtypes = dict(kbf=BF16, vbf=BF16, vr=BF16)
    names = ["qa", "ka", "va", "kbf", "vbf", "kmean", "qr", "kr", "vr", "gr", "ga", "gb"]
    out_specs, out_shapes = [], []
    for nm in names:
        if nm == "kmean":
            out_specs.append(pl.BlockSpec((tm // MOBA_BLOCK, 1, ATT_W), lambda i: (i, 0, 0)))
            out_shapes.append(jax.ShapeDtypeStruct((m // MOBA_BLOCK, 1, ATT_W), F32))
        else:
            out_specs.append(row(widths[nm]))
            out_shapes.append(jax.ShapeDtypeStruct((m, widths[nm]), dtypes.get(nm, F32)))
    outs = pl.pallas_call(
        _in_proj_kernel,
        grid=(m // tm,),
        in_specs=[row(D_MODEL), mod, mod, _resident((D_MODEL, D_IN))],
        out_specs=out_specs,
        out_shape=out_shapes,
        compiler_params=_params("arbitrary"),
        name="in_proj",
    )(x, shift, scale, w_in_bf16)
    return dict(zip(names, outs))


def t5_bucket(rel):
    n = jnp.maximum(rel, 0)
    max_exact = N_BUCKETS // 2
    ratio = jnp.log(jnp.maximum(n, max_exact).astype(F32) / max_exact) / math.log(MAX_DISTANCE / max_exact)
    large = jnp.minimum(max_exact + (ratio * (N_BUCKETS - max_exact)).astype(jnp.int32), N_BUCKETS - 1)
    return jnp.where(n < max_exact, n, large)


def _bias_from_buckets(bucket, rel_bias_ref, head):
    out = jnp.zeros(bucket.shape, F32)
    for b in range(N_BUCKETS):
        out = jnp.where(bucket == b, rel_bias_ref[b, head], out)
    return out


def _bias_rows_from_buckets(bucket, relb_rows):
    out = jnp.zeros(bucket.shape, F32)
    for b in range(N_BUCKETS):
        out = jnp.where(bucket == b, relb_rows[:, b:b + 1], out)
    return out


def _top_block_ids(scores, n_eligible, n_pick):
    n_blocks = scores.shape[1]
    blk = lax.broadcasted_iota(jnp.int32, scores.shape, 1)
    sc = jnp.where(blk < n_eligible, scores, NEG_INF)
    picks = []
    for _ in range(n_pick):
        best = jnp.max(sc, axis=1, keepdims=True)
        first = jnp.min(jnp.where(sc == best, blk, n_blocks), axis=1, keepdims=True)
        picks.append(jnp.where(first < n_eligible, first, -1))
        sc = jnp.where(blk == first, NEG_INF, sc)
    return picks


def _picked(picks, block_id):
    keep = picks[0] == block_id
    for p in picks[1:]:
        keep = jnp.logical_or(keep, p == block_id)
    return keep


def _moba_prompt_kernel(relb_ref, q_ref, k_ref, v_ref, kmean_ref, bkt_own_ref, bkt_prev_ref,
                        o_ref, bias_own, bias_prev):
    n_i = pl.program_id(0)
    kb = pl.program_id(1)
    blk_rows = MOBA_BLOCK

    @pl.when(jnp.logical_and(n_i == 0, kb == 0))
    def _():
        for h in range(H_A):
            bias_own[h] = _bias_from_buckets(bkt_own_ref[...], relb_ref, h)
            bias_prev[h] = _bias_from_buckets(bkt_prev_ref[...], relb_ref, h)

    q = q_ref[0]
    kmean = kmean_ref[0]
    row = lax.broadcasted_iota(jnp.int32, (blk_rows, blk_rows), 0)
    col = lax.broadcasted_iota(jnp.int32, (blk_rows, blk_rows), 1)
    causal = row >= col
    own_start = pl.multiple_of(kb * blk_rows, blk_rows)
    prev_blk = jnp.maximum(kb - 1, 0)
    prev_start = pl.multiple_of(prev_blk * blk_rows, blk_rows)
    head_lanes = [slice(h * HD_A, (h + 1) * HD_A) for h in range(H_A)]
    picks = [_top_block_ids(lax.dot_general(q[:, ln], kmean[:, ln], NT_DIMS, precision=HIGHEST,
                                            preferred_element_type=F32), kb, MOBA_TOPK)
             for ln in head_lanes]
    qs = [(q[:, ln] * (HD_A ** -0.5)).astype(BF16) for ln in head_lanes]

    def attend(h, k_start, bias, keep, carry):
        m, l, acc = carry
        kh = k_ref[0, pl.ds(k_start, blk_rows), head_lanes[h]]
        vh = v_ref[0, pl.ds(k_start, blk_rows), head_lanes[h]]
        s = lax.dot_general(qs[h], kh, NT_DIMS, preferred_element_type=F32) + bias
        s = jnp.where(keep, s, NEG_INF)
        m_new = jnp.maximum(m, jnp.max(s, axis=1, keepdims=True))
        alpha = jnp.exp(m - m_new)
        p = jnp.exp(s - m_new)
        l = alpha * l + jnp.sum(p, axis=1, keepdims=True)
        acc = alpha * acc + jnp.dot(p.astype(BF16), vh, preferred_element_type=F32)
        return m_new, l, acc

    carries = []
    for h in range(H_A):
        s0 = lax.dot_general(qs[h], k_ref[0, pl.ds(own_start, blk_rows), head_lanes[h]], NT_DIMS,
                             preferred_element_type=F32) + bias_own[h]
        s0 = jnp.where(causal, s0, NEG_INF)
        m0 = jnp.max(s0, axis=1, keepdims=True)
        p0 = jnp.exp(s0 - m0)
        carry = (m0, jnp.sum(p0, axis=1, keepdims=True),
                 jnp.dot(p0.astype(BF16), v_ref[0, pl.ds(own_start, blk_rows), head_lanes[h]],
                         preferred_element_type=F32))
        carries.append(attend(h, prev_start, bias_prev[h], _picked(picks[h], prev_blk), carry))

    def far_body(j, cs):
        k_start = pl.multiple_of(j * blk_rows, blk_rows)
        return tuple(attend(h, k_start, relb_ref[N_BUCKETS - 1, h], _picked(picks[h], j), cs[h])
                     for h in range(H_A))

    carries = lax.fori_loop(0, jnp.maximum(kb - 1, 0), far_body, tuple(carries))
    outs = [acc / l for (_, l, acc) in carries]
    o_ref[0] = jnp.concatenate(outs, axis=-1).astype(o_ref.dtype)


def moba_prompt(q, kbf, vbf, kmean, rel_bias):
    n, s, _ = q.shape
    nb = s // MOBA_BLOCK
    i = jnp.arange(MOBA_BLOCK)
    rel_own = i[:, None] - i[None, :]
    bkt_own = t5_bucket(rel_own).astype(jnp.int32)
    bkt_prev = t5_bucket(rel_own + MOBA_BLOCK).astype(jnp.int32)
    grid_spec = pltpu.PrefetchScalarGridSpec(
        num_scalar_prefetch=1,
        grid=(n, nb),
        in_specs=[pl.BlockSpec((1, MOBA_BLOCK, ATT_W), lambda a, b, *_: (a, b, 0)),
                  pl.BlockSpec((1, s, ATT_W), lambda a, b, *_: (a, 0, 0)),
                  pl.BlockSpec((1, s, ATT_W), lambda a, b, *_: (a, 0, 0)),
                  pl.BlockSpec((1, nb, ATT_W), lambda a, b, *_: (a, 0, 0)),
                  pl.BlockSpec((MOBA_BLOCK, MOBA_BLOCK), lambda a, b, *_: (0, 0)),
                  pl.BlockSpec((MOBA_BLOCK, MOBA_BLOCK), lambda a, b, *_: (0, 0))],
        out_specs=pl.BlockSpec((1, MOBA_BLOCK, ATT_W), lambda a, b, *_: (a, b, 0)),
        scratch_shapes=[pltpu.VMEM((H_A, MOBA_BLOCK, MOBA_BLOCK), F32),
                        pltpu.VMEM((H_A, MOBA_BLOCK, MOBA_BLOCK), F32)],
    )
    return pl.pallas_call(
        _moba_prompt_kernel,
        grid_spec=grid_spec,
        out_shape=jax.ShapeDtypeStruct((n, s, ATT_W), BF16),
        compiler_params=_params("arbitrary", "arbitrary"),
        name="moba_prompt",
    )(rel_bias, q, kbf, vbf, kmean, bkt_own, bkt_prev)


def _moba_sample_kernel(pt_ref, q_ref, knew_ref, vnew_ref, relb_rows_ref, bkt_last_ref, bkt_own_ref,
                        ck_hbm, cv_hbm, o_ref,
                        kbuf, vbuf, kown, vown, logit, prob, means, bias_last, bias_own, sems):
    seq = pl.program_id(0)
    n_seq = pl.num_programs(0)
    n_pages = kbuf.shape[0]
    n_blocks = n_pages // PAGES_PER_BLOCK
    t_new = q_ref.shape[1]
    n_rows = t_new * H_A

    def page_copy(cache, buf, sem_slot, s, j):
        return pltpu.make_async_copy(cache.at[pt_ref[s, j]], buf.at[j], sems.at[sem_slot])

    def start_pages(cache, buf, sem_slot, s):
        for j in range(n_pages):
            page_copy(cache, buf, sem_slot, s, j).start()

    def wait_pages(cache, buf, sem_slot, s):
        for j in range(n_pages):
            page_copy(cache, buf, sem_slot, s, j).wait()

    @pl.when(seq == 0)
    def _():
        start_pages(ck_hbm, kbuf, 0, 0)
        start_pages(cv_hbm, vbuf, 1, 0)
        kown[...] = jnp.zeros_like(kown)
        vown[...] = jnp.zeros_like(vown)
        means[...] = jnp.zeros_like(means)
        relb_rows = relb_rows_ref[...]
        bias_last[...] = _bias_rows_from_buckets(bkt_last_ref[...], relb_rows)
        bias_own[...] = _bias_rows_from_buckets(bkt_own_ref[...], relb_rows)

    q = q_ref[0]
    head_mask = (lax.broadcasted_iota(jnp.int32, (H_A, ATT_W), 1) // HD_A
                 == lax.broadcasted_iota(jnp.int32, (H_A, ATT_W), 0))
    qbd = jnp.concatenate([jnp.where(head_mask, jnp.broadcast_to(q[t:t + 1], (H_A, ATT_W)), 0.0)
                           for t in range(t_new)], axis=0)
    qs = (qbd * (HD_A ** -0.5)).astype(BF16)
    kown[0:t_new, :] = knew_ref[0]
    vown[0:t_new, :] = vnew_ref[0]

    wait_pages(ck_hbm, kbuf, 0, seq)

    blk_lane = lax.broadcasted_iota(jnp.int32, means.shape, 1)

    def k_block(b, carry):
        tot = jnp.zeros((ATT_W, PAGE_SIZE), F32)
        for pg in range(PAGES_PER_BLOCK):
            j = b * PAGES_PER_BLOCK + pg
            kp = kbuf[j]
            tot = tot + kp
            logit[j] = jnp.dot(qs, kp.astype(BF16), preferred_element_type=F32)
        mean_col = jnp.sum(tot, axis=1, keepdims=True) * (1.0 / MOBA_BLOCK)
        means[...] = jnp.where(blk_lane == b, mean_col, means[...])
        return carry

    lax.fori_loop(0, n_blocks, k_block, 0, unroll=2)

    @pl.when(seq + 1 < n_seq)
    def _():
        start_pages(ck_hbm, kbuf, 0, seq + 1)

    scores = jnp.dot(qbd, means[:, 0:n_blocks], precision=HIGHEST, preferred_element_type=F32)
    picks = _top_block_ids(scores, n_blocks, min(MOBA_TOPK, n_blocks))

    far_bias = relb_rows_ref[:, N_BUCKETS - 1:N_BUCKETS]
    own_ok = bkt_own_ref[...] >= 0
    s_own = lax.dot_general(qs, kown[...].astype(BF16), NT_DIMS, preferred_element_type=F32)
    s_own = jnp.where(own_ok, s_own + bias_own[...], NEG_INF)
    mx = s_own
    for j in range(n_pages):
        b, pg = divmod(j, PAGES_PER_BLOCK)
        if b == n_blocks - 1:
            bias = bias_last[:, pg * PAGE_SIZE:(pg + 1) * PAGE_SIZE]
        else:
            bias = far_bias
        s = jnp.where(_picked(picks, b), logit[j] + bias, NEG_INF)
        logit[j] = s
        mx = jnp.maximum(mx, s)
    m = jnp.max(mx, axis=1, keepdims=True)
    p_own = jnp.exp(s_own - m)
    lsum = p_own
    for j in range(n_pages):
        p = jnp.exp(logit[j] - m)
        prob[j] = p.astype(BF16)
        lsum = lsum + p
    l = jnp.sum(lsum, axis=1, keepdims=True)

    wait_pages(cv_hbm, vbuf, 1, seq)

    def v_block(b, acc):
        for pg in range(PAGES_PER_BLOCK):
            j = b * PAGES_PER_BLOCK + pg
            acc = acc + lax.dot_general(prob[j], vbuf[j].astype(BF16), NT_DIMS,
                                        preferred_element_type=F32)
        return acc

    acc = lax.fori_loop(0, n_blocks, v_block, jnp.zeros((n_rows, ATT_W), F32), unroll=4)

    @pl.when(seq + 1 < n_seq)
    def _():
        start_pages(cv_hbm, vbuf, 1, seq + 1)

    acc = acc + jnp.dot(p_own.astype(BF16), vown[...].astype(BF16), preferred_element_type=F32)
    acc = acc / l
    for t in range(t_new):
        rows = acc[t * H_A:(t + 1) * H_A]
        o_ref[0, t:t + 1, :] = jnp.sum(jnp.where(head_mask, rows, 0.0), axis=0, keepdims=True)


def moba_sample(q, k_new, v_new, cache_k, cache_v, page_table, rel_bias):
    n, t, _ = q.shape
    n_pages = page_table.shape[1]
    assert n_pages % PAGES_PER_BLOCK == 0 and PAGES_PER_BLOCK <= n_pages <= PAGE_SIZE * PAGES_PER_BLOCK
    n_rows = t * H_A
    t_of_row = jnp.arange(n_rows) // H_A
    relb_rows = jnp.tile(rel_bias.T, (t, 1))
    bkt_last = t5_bucket(MOBA_BLOCK + t_of_row[:, None] - jnp.arange(MOBA_BLOCK)[None, :]).astype(jnp.int32)
    cols = jnp.arange(PAGE_SIZE)[None, :]
    rel_own = t_of_row[:, None] - cols
    bkt_own = jnp.where((rel_own >= 0) & (cols < t), t5_bucket(rel_own), -1).astype(jnp.int32)
    seq_spec = pl.BlockSpec((1, t, ATT_W), lambda s, *_: (s, 0, 0))
    const = lambda shape: pl.BlockSpec(shape, lambda s, *_: (0,) * len(shape))
    grid_spec = pltpu.PrefetchScalarGridSpec(
        num_scalar_prefetch=1,
        grid=(n,),
        in_specs=[seq_spec, seq_spec, seq_spec,
                  const((n_rows, N_BUCKETS)), const((n_rows, MOBA_BLOCK)), const((n_rows, PAGE_SIZE)),
                  pl.BlockSpec(memory_space=pl.ANY), pl.BlockSpec(memory_space=pl.ANY)],
        out_specs=seq_spec,
        scratch_shapes=[pltpu.VMEM((n_pages, ATT_W, PAGE_SIZE), F32),
                        pltpu.VMEM((n_pages, ATT_W, PAGE_SIZE), F32),
                        pltpu.VMEM((PAGE_SIZE, ATT_W), F32),
                        pltpu.VMEM((PAGE_SIZE, ATT_W), F32),
                        pltpu.VMEM((n_pages, n_rows, PAGE_SIZE), F32),
                        pltpu.VMEM((n_pages, n_rows, PAGE_SIZE), BF16),
                        pltpu.VMEM((ATT_W, PAGE_SIZE), F32),
                        pltpu.VMEM((n_rows, MOBA_BLOCK), F32),
                        pltpu.VMEM((n_rows, PAGE_SIZE), F32),
                        pltpu.SemaphoreType.DMA((2,))],
    )
    return pl.pallas_call(
        _moba_sample_kernel,
        grid_spec=grid_spec,
        out_shape=jax.ShapeDtypeStruct((n, t, ATT_W), F32),
        compiler_params=_params("arbitrary"),
        name="moba_sample",
    )(page_table, q, k_new, v_new, relb_rows, bkt_last, bkt_own, cache_k, cache_v)


def _swap_pairs(x):
    lane = lax.broadcasted_iota(jnp.int32, x.shape, 1)
    width = x.shape[1]
    return jnp.where(lane % 2 == 0, pltpu.roll(x, width - 1, 1), pltpu.roll(x, 1, 1))


def _retention_kernel(q_ref, k_ref, v_ref, g_ref, cos_ref, sin_ref, decay_ref, cross_ref, kw_ref,
                      sdec_ref, ng_ref, nb_ref, st_in_ref, y_ref, st_out_ref, st):
    c = pl.program_id(1)

    @pl.when(c == 0)
    def _():
        st[...] = st_in_ref[0]

    cos = cos_ref[...]
    sin = sin_ref[...]
    q = q_ref[0]
    k = k_ref[0]
    v = v_ref[0]
    g = g_ref[0]
    for h in range(H_R):
        lanes = slice(h * DK_R, (h + 1) * DK_R)
        qh = q[:, lanes]
        kh = k[:, lanes]
        qh = (qh * cos + _swap_pairs(qh) * sin).astype(BF16)
        kh = (kh * cos + _swap_pairs(kh) * sin) * (DK_R ** -0.5)
        vh = v[:, lanes]
        state = st[h]
        scores = lax.dot_general(qh, kh.astype(BF16), NT_DIMS, preferred_element_type=F32) * decay_ref[h]
        inner = jnp.dot(scores.astype(BF16), vh, preferred_element_type=F32)
        cross = jnp.dot(qh, state.astype(BF16), preferred_element_type=F32) * cross_ref[:, lanes]
        kd = (kh * kw_ref[:, lanes]).astype(BF16)
        st[h] = sdec_ref[h] * state + lax.dot_general(kd, vh, TN_DIMS, preferred_element_type=F32)
        o = inner + cross
        o = _layer_norm(o) * ng_ref[:, lanes] + nb_ref[:, lanes]
        y_ref[0, :, lanes] = (_silu(g[:, lanes]) * o).astype(y_ref.dtype)

    @pl.when(c == pl.num_programs(1) - 1)
    def _():
        st_out_ref[0] = st[...]


def _retention_tables(pos, n_valid, chunk):
    theta = 1.0 / (ROPE_BASE ** jnp.linspace(0.0, 1.0, DK_R // 2, dtype=F32))
    ang = pos.astype(F32)[:, None] * theta[None, :]
    cos = jnp.repeat(jnp.cos(ang), 2, axis=1)
    sin = jnp.repeat(jnp.sin(ang), 2, axis=1) * jnp.tile(jnp.array([-1.0, 1.0], F32), DK_R // 2)[None, :]
    log_g = jnp.log1p(-jnp.exp2(-5.0 - jnp.arange(H_R, dtype=F32)))
    i = jnp.arange(chunk, dtype=F32)
    valid = i < n_valid
    diff = i[:, None] - i[None, :]
    decay = jnp.where(diff >= 0, jnp.exp(jnp.maximum(diff, 0.0)[None] * log_g[:, None, None]), 0.0)
    decay = jnp.where(valid[None, :, None] & valid[None, None, :], decay, 0.0)
    cross = jnp.where(valid[:, None], jnp.exp((i + 1.0)[:, None] * log_g[None, :]), 0.0)
    kw = jnp.where(valid[:, None], jnp.exp((n_valid - 1.0 - i)[:, None] * log_g[None, :]), 0.0)
    sdec = jnp.exp(n_valid * log_g)
    return dict(cos=cos, sin=sin, decay=decay,
                cross=jnp.repeat(cross, DV_R, axis=1), kw=jnp.repeat(kw, DK_R, axis=1),
                sdec=jnp.broadcast_to(sdec[:, None, None], (H_R, 1, DV_R)))


def retention(q, k, v, g, state0, tables, norm_g, norm_b):
    n, s, _ = q.shape
    chunk = tables["decay"].shape[1]
    n_chunks = s // chunk
    seq = pl.BlockSpec((1, chunk, RET_QK_W), lambda a, c: (a, c, 0))
    per_chunk = pl.BlockSpec((chunk, DK_R), lambda a, c: (c, 0))
    per_row = pl.BlockSpec((chunk, RET_QK_W), lambda a, c: (0, 0))
    st_spec = pl.BlockSpec((1, H_R, DK_R, DV_R), lambda a, c: (a, 0, 0, 0))
    vec = pl.BlockSpec((1, RET_V_W), lambda a, c: (0, 0))
    return pl.pallas_call(
        _retention_kernel,
        grid=(n, n_chunks),
        in_specs=[seq, seq, seq, seq, per_chunk, per_chunk,
                  pl.BlockSpec((H_R, chunk, chunk), lambda a, c: (0, 0, 0)),
                  per_row, per_row,
                  pl.BlockSpec((H_R, 1, DV_R), lambda a, c: (0, 0, 0)),
                  vec, vec, st_spec],
        out_specs=[seq, st_spec],
        out_shape=[jax.ShapeDtypeStruct((n, s, RET_V_W), BF16),
                   jax.ShapeDtypeStruct((n, H_R, DK_R, DV_R), F32)],
        scratch_shapes=[pltpu.VMEM((H_R, DK_R, DV_R), F32)],
        compiler_params=_params("arbitrary", "arbitrary"),
        name="retention",
    )(q, k, v, g, tables["cos"], tables["sin"], tables["decay"], tables["cross"], tables["kw"],
      tables["sdec"], norm_g.reshape(1, -1), norm_b.reshape(1, -1), state0)


def _mix_kernel(att_ref, ret_ref, ga_ref, gb_ref, x_ref, gate1_ref, shift2_ref, scale2_ref,
                wa_ref, wb_ref, wo_ref, g1_ref, b1_ref, wr_ref, x1_ref, h2_ref, h2w_ref, logit_ref):
    ya = jnp.dot(att_ref[...], wa_ref[...], preferred_element_type=F32)
    yb = jnp.dot(ret_ref[...], wb_ref[...], preferred_element_type=F32)
    mixed_in = jax.nn.sigmoid(ga_ref[...]) * ya + jax.nn.sigmoid(gb_ref[...]) * yb
    mixed = jnp.dot(mixed_in.astype(BF16), wo_ref[...], preferred_element_type=F32)
    x1 = _layer_norm(ALPHA * x_ref[...] + gate1_ref[0] * mixed) * g1_ref[...] + b1_ref[...]
    x1_ref[...] = x1
    h2 = _layer_norm(x1) * (1.0 + scale2_ref[0]) + shift2_ref[0]
    h2_bf = h2.astype(BF16)
    h2_ref[...] = h2_bf
    h2w_ref[...] = _pack_bf16_halves(h2_bf)
    logit_ref[...] = jnp.dot(h2, wr_ref[...], precision=HIGHEST, preferred_element_type=F32)


def mix_out(att, ret, ga, gb, x, gate1, shift2, scale2, wa, wb, wo, ln_g, ln_b, w_router, tiles_per_group):
    m = x.shape[0]
    tm = ROW_TILE
    r = gate1.shape[1]
    row = lambda w: pl.BlockSpec((tm, w), lambda i: (i, 0))
    mod = pl.BlockSpec((1, r, D_MODEL), lambda i: (i // tiles_per_group, 0, 0))
    return pl.pallas_call(
        _mix_kernel,
        grid=(m // tm,),
        in_specs=[row(ATT_W), row(RET_V_W), row(D_MODEL), row(D_MODEL), row(D_MODEL), mod, mod, mod,
                  _resident((ATT_W, D_MODEL)), _resident((RET_V_W, D_MODEL)), _resident((D_MODEL, D_MODEL)),
                  _resident((1, D_MODEL)), _resident((1, D_MODEL)), _resident((D_MODEL, N_EXPERTS))],
        out_specs=[row(D_MODEL), row(D_MODEL), row(D_MODEL // 2), row(N_EXPERTS)],
        out_shape=[jax.ShapeDtypeStruct((m, D_MODEL), F32), jax.ShapeDtypeStruct((m, D_MODEL), BF16),
                   jax.ShapeDtypeStruct((m, D_MODEL // 2), jnp.uint32),
                   jax.ShapeDtypeStruct((m, N_EXPERTS), F32)],
        compiler_params=_params("arbitrary"),
        name="mix_out",
    )(att, ret, ga, gb, x, gate1, shift2, scale2, wa, wb, wo, ln_g.reshape(1, -1), ln_b.reshape(1, -1),
      w_router)


def _first_max(x, ids, sentinel):
    best = jnp.max(x, axis=0, keepdims=True)
    first = jnp.min(jnp.where(x == best, ids, sentinel), axis=0, keepdims=True)
    return best, first


def _route_kernel(logit_ref, bias_ref, idx_ref, gate_ref):
    s = jax.nn.sigmoid(logit_ref[...].T)
    sb = s + bias_ref[...]
    tm = s.shape[1]
    gid = lax.broadcasted_iota(jnp.int32, (GROUP_SIZE, tm), 0)
    grp_scores = []
    for gi in range(N_GROUPS):
        grp = sb[gi * GROUP_SIZE:(gi + 1) * GROUP_SIZE]
        top1, first = _first_max(grp, gid, GROUP_SIZE)
        top2 = jnp.max(jnp.where(gid == first, NEG_INF, grp), axis=0, keepdims=True)
        grp_scores.append(top1 + top2)
    gs = jnp.concatenate(grp_scores, axis=0)
    gids = lax.broadcasted_iota(jnp.int32, (N_GROUPS, tm), 0)
    gmask = jnp.zeros((N_GROUPS, tm), jnp.bool_)
    for _ in range(TOPK_GROUPS):
        _, first = _first_max(gs, gids, N_GROUPS)
        hit = gids == first
        gmask = jnp.logical_or(gmask, hit)
        gs = jnp.where(hit, NEG_INF, gs)
    masked = jnp.concatenate(
        [jnp.where(gmask[gi:gi + 1], sb[gi * GROUP_SIZE:(gi + 1) * GROUP_SIZE], NEG_INF)
         for gi in range(N_GROUPS)], axis=0)
    eids = lax.broadcasted_iota(jnp.int32, (N_EXPERTS, tm), 0)
    picks, weights = [], []
    for _ in range(TOP_K):
        _, first = _first_max(masked, eids, N_EXPERTS)
        hit = eids == first
        picks.append(first)
        weights.append(jnp.sum(jnp.where(hit, s, 0.0), axis=0, keepdims=True))
        masked = jnp.where(hit, NEG_INF, masked)
    w = jnp.concatenate(weights, axis=0)
    idx_ref[...] = jnp.concatenate(picks, axis=0)
    gate_ref[...] = w / jnp.sum(w, axis=0, keepdims=True) * ROUTED_SCALE


def route(logits, router_bias):
    m = logits.shape[0]
    tm = ROW_TILE
    return pl.pallas_call(
        _route_kernel,
        grid=(m // tm,),
        in_specs=[pl.BlockSpec((tm, N_EXPERTS), lambda i: (i, 0)),
                  pl.BlockSpec((N_EXPERTS, 1), lambda i: (0, 0))],
        out_specs=[pl.BlockSpec((TOP_K, tm), lambda i: (0, i)), pl.BlockSpec((TOP_K, tm), lambda i: (0, i))],
        out_shape=[jax.ShapeDtypeStruct((TOP_K, m), jnp.int32), jax.ShapeDtypeStruct((TOP_K, m), F32)],
        compiler_params=_params("arbitrary"),
        name="route",
    )(logits, router_bias.reshape(-1, 1))


def _expert_kernel(blk_e_ref, n_used_ref, tok_ref, wg_ref, wu_ref, wd_ref, hw_ref, y_ref, xw):
    i = pl.program_id(0)
    n_used = n_used_ref[0]
    tm = xw.shape[0]

    @pl.when(i < n_used)
    def _():
        for r in range(tm):
            xw[r:r + 1, :] = hw_ref[pl.ds(tok_ref[0, 0, r], 1), :]
        x = _unpack_bf16_halves(xw[...])
        a = (_silu(jnp.dot(x, wg_ref[0].astype(BF16), preferred_element_type=F32))
             * jnp.dot(x, wu_ref[0].astype(BF16), preferred_element_type=F32))
        y_ref[...] = jnp.dot(a.astype(BF16), wd_ref[0].astype(BF16), preferred_element_type=F32)

    @pl.when(i >= n_used)
    def _():
        y_ref[...] = jnp.zeros_like(y_ref)


def routed_experts(h_words, buf_tok, blk_e, n_used, w_gate, w_up, w_down):
    n_blocks, _, tm = buf_tok.shape
    m, half = h_words.shape
    grid_spec = pltpu.PrefetchScalarGridSpec(
        num_scalar_prefetch=2,
        grid=(n_blocks,),
        in_specs=[pl.BlockSpec((1, 1, tm), lambda i, be, nu: (i, 0, 0), memory_space=pltpu.SMEM),
                  pl.BlockSpec((1, D_MODEL, D_EXPERT), lambda i, be, nu: (be[i], 0, 0)),
                  pl.BlockSpec((1, D_MODEL, D_EXPERT), lambda i, be, nu: (be[i], 0, 0)),
                  pl.BlockSpec((1, D_EXPERT, D_MODEL), lambda i, be, nu: (be[i], 0, 0)),
                  _resident((m, half))],
        out_specs=pl.BlockSpec((tm, D_MODEL), lambda i, be, nu: (i, 0)),
        scratch_shapes=[pltpu.VMEM((tm, half), jnp.uint32)],
    )
    return pl.pallas_call(
        _expert_kernel,
        grid_spec=grid_spec,
        out_shape=jax.ShapeDtypeStruct((n_blocks * tm, D_MODEL), F32),
        compiler_params=_params("arbitrary"),
        name="routed_experts",
    )(blk_e, n_used, buf_tok, w_gate, w_up, w_down, h_words)


def _final_kernel(dest_ref, dest_next_ref, x1_ref, h2_ref, gate_ref, gate2_ref, wsg_ref, wsu_ref, wsd_ref,
                  g2_ref, b2_ref, ys_hbm, y_ref, rbuf, sems):
    i = pl.program_id(0)
    n_steps = pl.num_programs(0)
    tm = x1_ref.shape[0]
    n_gather = TOP_K * tm
    slot = i % 2

    def row_copy(dest_smem, r, to_slot):
        return pltpu.make_async_copy(ys_hbm.at[pl.ds(dest_smem[0, 0, r], 1)],
                                     rbuf.at[to_slot, pl.ds(r, 1)], sems.at[to_slot])

    def start_rows(dest_smem, to_slot):
        unroll = 8

        def body(g, c):
            for u in range(unroll):
                row_copy(dest_smem, g * unroll + u, to_slot).start(priority=u % 2)
            return c
        lax.fori_loop(0, n_gather // unroll, body, 0)

    def wait_rows(to_slot):
        pltpu.make_async_copy(ys_hbm.at[pl.ds(0, n_gather)], rbuf.at[to_slot], sems.at[to_slot]).wait()

    @pl.when(i == 0)
    def _():
        start_rows(dest_ref, 0)

    @pl.when(i + 1 < n_steps)
    def _():
        start_rows(dest_next_ref, 1 - slot)

    h = h2_ref[...]
    a = (_silu(jnp.dot(h, wsg_ref[...], preferred_element_type=F32))
         * jnp.dot(h, wsu_ref[...], preferred_element_type=F32))
    f = jnp.dot(a.astype(BF16), wsd_ref[...], preferred_element_type=F32)
    wait_rows(slot)
    gate = gate_ref[...]
    routed = jnp.zeros((tm, D_MODEL), F32)
    for k in range(TOP_K):
        routed = routed + rbuf[slot, k * tm:(k + 1) * tm, :] * gate[:, k:k + 1]
    f = routed + f
    y_ref[...] = _layer_norm(ALPHA * x1_ref[...] + gate2_ref[...] * f) * g2_ref[...] + b2_ref[...]


def final_out(x1, h2, gate, gate2_rows, dest, ys, wsg, wsu, wsd, ln_g, ln_b):
    m = x1.shape[0]
    tm = COMBINE_TILE
    n_steps = m // tm
    row = lambda w: pl.BlockSpec((tm, w), lambda i: (i, 0))
    dest_spec = lambda f: pl.BlockSpec((1, 1, TOP_K * tm), lambda i: (f(i), 0, 0), memory_space=pltpu.SMEM)
    return pl.pallas_call(
        _final_kernel,
        grid=(n_steps,),
        in_specs=[dest_spec(lambda i: i), dest_spec(lambda i: jnp.minimum(i + 1, n_steps - 1)),
                  row(D_MODEL), row(D_MODEL), row(TOP_K), row(D_MODEL),
                  _resident((D_MODEL, D_SHARED)), _resident((D_MODEL, D_SHARED)), _resident((D_SHARED, D_MODEL)),
                  _resident((1, D_MODEL)), _resident((1, D_MODEL)),
                  pl.BlockSpec(memory_space=pl.ANY)],
        out_specs=row(D_MODEL),
        out_shape=jax.ShapeDtypeStruct((m, D_MODEL), F32),
        scratch_shapes=[pltpu.VMEM((2, TOP_K * tm, D_MODEL), F32), pltpu.SemaphoreType.DMA((2,))],
        compiler_params=_params("arbitrary"),
        name="final_out",
    )(dest, dest, x1, h2, gate, gate2_rows, wsg, wsu, wsd, ln_g.reshape(1, -1), ln_b.reshape(1, -1), ys)


def _dispatch(idx_t):
    m = idx_t.shape[1]
    bm = MOE_ROW_TILE
    n_pairs = m * TOP_K
    experts = jnp.arange(N_EXPERTS, dtype=jnp.int32)
    pair_ids = jnp.arange(n_pairs, dtype=jnp.int32)
    flat_e = idx_t.T.reshape(-1)
    se, order = lax.sort((flat_e, pair_ids), num_keys=1, is_stable=True)
    counts = jnp.sum((flat_e[:, None] == experts[None, :]).astype(jnp.int32), axis=0)
    padded = (counts + bm - 1) // bm * bm
    pad_end = jnp.cumsum(padded)
    pad_start = pad_end - padded
    start = jnp.cumsum(counts) - counts
    shift = (pad_start - start).astype(jnp.int32)
    dest_sorted = pair_ids + jnp.sum(jnp.where(se[:, None] == experts[None, :], shift[None, :], 0), axis=1)
    n_blocks = -(-(n_pairs + N_EXPERTS * (bm - 1)) // bm)
    n_rows = n_blocks * bm
    buf_tok = jnp.zeros((n_rows,), jnp.int32).at[dest_sorted].set(
        order // TOP_K, indices_are_sorted=True, unique_indices=True)
    _, dest = lax.sort((order, dest_sorted), num_keys=1)
    blk_e = jnp.minimum(jnp.searchsorted(pad_end, jnp.arange(n_blocks) * bm, side='right'),
                        N_EXPERTS - 1).astype(jnp.int32)
    n_used = (pad_end[-1] // bm).astype(jnp.int32).reshape(1)
    tc = COMBINE_TILE
    dest_tiles = dest.reshape(m // tc, tc, TOP_K).transpose(0, 2, 1).reshape(m // tc, 1, TOP_K * tc)
    return buf_tok.reshape(n_blocks, 1, bm), blk_e, n_used, dest_tiles


def _expand_rows(v, reps):
    return jnp.repeat(v, reps, axis=0).reshape(-1, ROW_TILE, v.shape[-1])


def kernel(x_prompt, x_sample, cache_k, cache_v, state_ret, page_table, c_prompt, c_sample, w_ada, b_ada, w_in, rel_bias, ret_norm_g, ret_norm_b, w_branch_a, w_branch_b, w_out, ln1_g, ln1_b, w_router, router_bias, w_gate, w_up, w_down, w_shared_gate, w_shared_up, w_shared_down, ln2_g, ln2_b):
    assert DEPTH == 1
    l = 0
    nb, s, d = x_prompt.shape
    ns, t, _ = x_sample.shape
    past_len = page_table.shape[1] * PAGE_SIZE
    mp, ms = nb * s, ns * t

    c_all = jnp.concatenate([c_prompt, c_sample], axis=0)
    n_c = c_all.shape[0]
    c_all = jnp.pad(c_all, ((0, -n_c % 8), (0, 0)))
    mod = ada_modulation(c_all, w_ada[l], b_ada[l])
    mods_p = [mod[:nb, i * d:(i + 1) * d].reshape(nb, 1, d) for i in range(6)]
    mods_s = [_expand_rows(mod[nb:nb + ns, i * d:(i + 1) * d], t) for i in range(6)]
    tiles_p = s // ROW_TILE

    w_in_bf = w_in[l].astype(BF16)
    pp = in_proj(x_prompt.reshape(mp, d), mods_p[0], mods_p[1], w_in_bf, tiles_p)
    ps = in_proj(x_sample.reshape(ms, d), mods_s[0], mods_s[1], w_in_bf, 1)

    att_p = moba_prompt(pp["qa"].reshape(nb, s, ATT_W), pp["kbf"].reshape(nb, s, ATT_W),
                        pp["vbf"].reshape(nb, s, ATT_W), pp["kmean"].reshape(nb, s // MOBA_BLOCK, ATT_W),
                        rel_bias)
    tab_p = _retention_tables(jnp.arange(s), float(RET_CHUNK), RET_CHUNK)
    ret_p, st_p = retention(pp["qr"].reshape(nb, s, -1), pp["kr"].reshape(nb, s, -1),
                            pp["vr"].reshape(nb, s, -1), pp["gr"].reshape(nb, s, -1),
                            jnp.zeros((nb, H_R, DK_R, DV_R), F32), tab_p, ret_norm_g[l], ret_norm_b[l])

    n_pool = cache_k.shape[1]
    page_major = lambda c: jnp.transpose(c[l], (0, 2, 3, 1)).reshape(n_pool, ATT_W, PAGE_SIZE)
    att_s = moba_sample(ps["qa"].reshape(ns, t, ATT_W), ps["ka"].reshape(ns, t, ATT_W),
                        ps["va"].reshape(ns, t, ATT_W), page_major(cache_k), page_major(cache_v),
                        page_table, rel_bias)
    pad_t = lambda a: jnp.pad(a.reshape(ns, t, -1), ((0, 0), (0, RET_CHUNK - t), (0, 0)))
    pos_s = past_len + jnp.arange(RET_CHUNK)
    tab_s = _retention_tables(pos_s, float(t), RET_CHUNK)
    ret_s, st_s = retention(pad_t(ps["qr"]), pad_t(ps["kr"]), pad_t(ps["vr"]), pad_t(ps["gr"]),
                            state_ret[l], tab_s, ret_norm_g[l], ret_norm_b[l])
    ret_s = ret_s[:, :t].reshape(ms, RET_V_W)

    wa, wb, wo = w_branch_a[l].astype(BF16), w_branch_b[l].astype(BF16), w_out[l].astype(BF16)
    x1_p, h2_p, hw_p, lg_p = mix_out(att_p.reshape(mp, ATT_W), ret_p.reshape(mp, RET_V_W), pp["ga"],
                                     pp["gb"], x_prompt.reshape(mp, d), mods_p[2], mods_p[3], mods_p[4],
                                     wa, wb, wo, ln1_g[l], ln1_b[l], w_router[l], tiles_p)
    x1_s, h2_s, hw_s, lg_s = mix_out(att_s.reshape(ms, ATT_W).astype(BF16), ret_s, ps["ga"], ps["gb"],
                                     x_sample.reshape(ms, d), mods_s[2], mods_s[3], mods_s[4],
                                     wa, wb, wo, ln1_g[l], ln1_b[l], w_router[l], 1)

    x1 = jnp.concatenate([x1_p, x1_s], axis=0)
    h2 = jnp.concatenate([h2_p, h2_s], axis=0)
    idx_t, gate_t = route(jnp.concatenate([lg_p, lg_s], axis=0), router_bias[l])
    buf_tok, blk_e, n_used, dest_tiles = _dispatch(idx_t)
    ys = routed_experts(jnp.concatenate([hw_p, hw_s], axis=0), buf_tok, blk_e, n_used,
                        w_gate[l], w_up[l], w_down[l])
    gate2_rows = jnp.concatenate([jnp.broadcast_to(mods_p[5], (nb, s, d)).reshape(mp, d),
                                  mods_s[5].reshape(ms, d)], axis=0)
    wsg, wsu, wsd = (w_shared_gate[l].astype(BF16), w_shared_up[l].astype(BF16),
                     w_shared_down[l].astype(BF16))
    y = final_out(x1, h2, gate_t.T, gate2_rows, dest_tiles, ys, wsg, wsu, wsd, ln2_g[l], ln2_b[l])

    return (y[:mp].reshape(nb, s, d), y[mp:].reshape(ns, t, d),
            pp["ka"].reshape(1, nb, s, H_A, HD_A), pp["va"].reshape(1, nb, s, H_A, HD_A), st_p[None],
            ps["ka"].reshape(1, ns, t, H_A, HD_A), ps["va"].reshape(1, ns, t, H_A, HD_A), st_s[None])
```

```python
import math

import jax
import jax.numpy as jnp
from jax import lax
from jax.experimental import pallas as pl
from jax.experimental.pallas import tpu as pltpu

D_MODEL = 1024
DEPTH = 1
PAGE_SIZE = 128
H_A = 8
HD_A = 64
ATT_W = H_A * HD_A
MOBA_BLOCK = 256
MOBA_TOPK = 3
PAGES_PER_BLOCK = MOBA_BLOCK // PAGE_SIZE
N_BUCKETS = 32
MAX_DISTANCE = 128
H_R = 4
DK_R = 128
DV_R = 128
RET_QK_W = H_R * DK_R
RET_V_W = H_R * DV_R
RET_CHUNK = 128
ROPE_BASE = 10000.0
N_EXPERTS = 256
TOP_K = 8
N_GROUPS = 8
TOPK_GROUPS = 4
GROUP_SIZE = N_EXPERTS // N_GROUPS
D_EXPERT = 256
D_SHARED = 256
ROUTED_SCALE = 2.5
ALPHA = (2.0 * DEPTH) ** 0.25
LN_EPS = 1e-5
D_IN = 3 * ATT_W + 2 * RET_QK_W + 2 * RET_V_W + 2 * D_MODEL

F32 = jnp.float32
BF16 = jnp.bfloat16
NEG_INF = float("-inf")
HIGHEST = lax.Precision.HIGHEST
NT_DIMS = (((1,), (1,)), ((), ()))
TN_DIMS = (((0,), (0,)), ((), ()))

VMEM_LIMIT_BYTES = 56 * 1024 * 1024
ROW_TILE = 512
MOE_ROW_TILE = 256
COMBINE_TILE = 256


def _params(*semantics):
    return pltpu.CompilerParams(dimension_semantics=semantics, vmem_limit_bytes=VMEM_LIMIT_BYTES)


def _layer_norm(x):
    mu = jnp.mean(x, axis=-1, keepdims=True)
    xc = x - mu
    var = jnp.mean(xc * xc, axis=-1, keepdims=True)
    return xc * lax.rsqrt(var + LN_EPS)


def _silu(x):
    return x * jax.nn.sigmoid(x)


def _pack_bf16_halves(x):
    bits = lax.bitcast_convert_type(x.astype(F32), jnp.uint32)
    w = x.shape[1] // 2
    return bits[:, w:] | (bits[:, :w] >> 16)


def _unpack_bf16_halves(words):
    lo = lax.bitcast_convert_type(words << 16, F32)
    hi = lax.bitcast_convert_type(words & jnp.uint32(0xFFFF0000), F32)
    return jnp.concatenate([lo, hi], axis=1).astype(BF16)


def _resident(shape):
    zeros = (0,) * len(shape)
    return pl.BlockSpec(shape, lambda *_: zeros, pipeline_mode=pl.Buffered(1))


def _ada_kernel(c_ref, w_ref, b_ref, o_ref):
    a = _silu(c_ref[...])
    o_ref[...] = jnp.dot(a, w_ref[...], precision=HIGHEST, preferred_element_type=F32) + b_ref[...]


def ada_modulation(c, w_ada, b_ada):
    n = c.shape[0]
    tn = 1536
    return pl.pallas_call(
        _ada_kernel,
        grid=(6 * D_MODEL // tn,),
        in_specs=[pl.BlockSpec((n, D_MODEL), lambda j: (0, 0)),
                  pl.BlockSpec((D_MODEL, tn), lambda j: (0, j)),
                  pl.BlockSpec((1, tn), lambda j: (0, j))],
        out_specs=pl.BlockSpec((n, tn), lambda j: (0, j)),
        out_shape=jax.ShapeDtypeStruct((n, 6 * D_MODEL), F32),
        compiler_params=_params("arbitrary"),
        name="ada_modulation",
    )(c, w_ada, b_ada.reshape(1, -1))


def _in_proj_kernel(x_ref, shift_ref, scale_ref, w_ref,
                    qa_ref, ka_ref, va_ref, kbf_ref, vbf_ref, kmean_ref,
                    qr_ref, kr_ref, vr_ref, gr_ref, ga_ref, gb_ref):
    h = (_layer_norm(x_ref[...]) * (1.0 + scale_ref[0]) + shift_ref[0]).astype(BF16)

    def proj(start, width):
        return jnp.dot(h, w_ref[:, start:start + width], preferred_element_type=F32)

    qa_ref[...] = proj(0, ATT_W)
    ka = proj(ATT_W, ATT_W)
    ka_ref[...] = ka
    kbf_ref[...] = ka.astype(BF16)
    n_blk = ka.shape[0] // MOBA_BLOCK
    for b in range(n_blk):
        kmean_ref[b] = jnp.mean(ka[b * MOBA_BLOCK:(b + 1) * MOBA_BLOCK], axis=0, keepdims=True)
    va = proj(2 * ATT_W, ATT_W)
    va_ref[...] = va
    vbf_ref[...] = va.astype(BF16)
    off = 3 * ATT_W
    qr_ref[...] = proj(off, RET_QK_W)
    kr_ref[...] = proj(off + RET_QK_W, RET_QK_W)
    off += 2 * RET_QK_W
    vr_ref[...] = proj(off, RET_V_W).astype(BF16)
    gr_ref[...] = proj(off + RET_V_W, RET_V_W)
    off += 2 * RET_V_W
    ga_ref[...] = proj(off, D_MODEL)
    gb_ref[...] = proj(off + D_MODEL, D_MODEL)


def in_proj(x, shift, scale, w_in_bf16, tiles_per_group):
    m = x.shape[0]
    tm = ROW_TILE
    r = shift.shape[1]
    row = lambda w: pl.BlockSpec((tm, w), lambda i: (i, 0))
    mod = pl.BlockSpec((1, r, D_MODEL), lambda i: (i // tiles_per_group, 0, 0))
    widths = dict(qa=ATT_W, ka=ATT_W, va=ATT_W, kbf=ATT_W, vbf=ATT_W, qr=RET_QK_W, kr=RET_QK_W,
                  vr=RET_V_W, gr=RET_V_W, ga=D_MODEL, gb=D_MODEL)
    dtypes = dict(kbf=BF16, vbf=BF16, vr=BF16)
    names = ["qa", "ka", "va", "kbf", "vbf", "kmean", "qr", "kr", "vr", "gr", "ga", "gb"]
    out_specs, out_shapes = [], []
    for nm in names:
        if nm == "kmean":
            out_specs.append(pl.BlockSpec((tm // MOBA_BLOCK, 1, ATT_W), lambda i: (i, 0, 0)))
            out_shapes.append(jax.ShapeDtypeStruct((m // MOBA_BLOCK, 1, ATT_W), F32))
        else:
            out_specs.append(row(widths[nm]))
            out_shapes.append(jax.ShapeDtypeStruct((m, widths[nm]), dtypes.get(nm, F32)))
    outs = pl.pallas_call(
        _in_proj_kernel,
        grid=(m // tm,),
        in_specs=[row(D_MODEL), mod, mod, _resident((D_MODEL, D_IN))],
        out_specs=out_specs,
        out_shape=out_shapes,
        compiler_params=_params("arbitrary"),
        name="in_proj",
    )(x, shift, scale, w_in_bf16)
    return dict(zip(names, outs))


def t5_bucket(rel):
    n = jnp.maximum(rel, 0)
    max_exact = N_BUCKETS // 2
    ratio = jnp.log(jnp.maximum(n, max_exact).astype(F32) / max_exact) / math.log(MAX_DISTANCE / max_exact)
    large = jnp.minimum(max_exact + (ratio * (N_BUCKETS - max_exact)).astype(jnp.int32), N_BUCKETS - 1)
    return jnp.where(n < max_exact, n, large)


def _bias_from_buckets(bucket, rel_bias_ref, head):
    out = jnp.zeros(bucket.shape, F32)
    for b in range(N_BUCKETS):
        out = jnp.where(bucket == b, rel_bias_ref[b, head], out)
    return out


def _bias_rows_from_buckets(bucket, relb_rows):
    out = jnp.zeros(bucket.shape, F32)
    for b in range(N_BUCKETS):
        out = jnp.where(bucket == b, relb_rows[:, b:b + 1], out)
    return out


def _top_block_ids(scores, n_eligible, n_pick, axis=1):
    n_blocks = scores.shape[axis]
    blk = lax.broadcasted_iota(jnp.int32, scores.shape, axis)
    sc = jnp.where(blk < n_eligible, scores, NEG_INF)
    picks = []
    for _ in range(n_pick):
        best = jnp.max(sc, axis=axis, keepdims=True)
        first = jnp.min(jnp.where(sc == best, blk, n_blocks), axis=axis, keepdims=True)
        picks.append(jnp.where(first < n_eligible, first, -1))
        sc = jnp.where(blk == first, NEG_INF, sc)
    return picks


def _picked(picks, block_id):
    keep = picks[0] == block_id
    for p in picks[1:]:
        keep = jnp.logical_or(keep, p == block_id)
    return keep


def _moba_prompt_kernel(relb_ref, q_ref, k_ref, v_ref, kmean_ref, bkt_own_ref, bkt_prev_ref,
                        o_ref, bias_own, bias_prev):
    n_i = pl.program_id(0)
    kb = pl.program_id(1)
    blk_rows = MOBA_BLOCK

    @pl.when(jnp.logical_and(n_i == 0, kb == 0))
    def _():
        for h in range(H_A):
            bias_own[h] = _bias_from_buckets(bkt_own_ref[...], relb_ref, h)
            bias_prev[h] = _bias_from_buckets(bkt_prev_ref[...], relb_ref, h)

    q = q_ref[0]
    kmean = kmean_ref[0]
    key_i = lax.broadcasted_iota(jnp.int32, (blk_rows, blk_rows), 0)
    qry_i = lax.broadcasted_iota(jnp.int32, (blk_rows, blk_rows), 1)
    causal = qry_i >= key_i
    own_start = pl.multiple_of(kb * blk_rows, blk_rows)
    prev_blk = jnp.maximum(kb - 1, 0)
    prev_start = pl.multiple_of(prev_blk * blk_rows, blk_rows)
    head_lanes = [slice(h * HD_A, (h + 1) * HD_A) for h in range(H_A)]
    picks = [_top_block_ids(lax.dot_general(kmean[:, ln], q[:, ln], NT_DIMS, precision=HIGHEST,
                                            preferred_element_type=F32), kb, MOBA_TOPK, axis=0)
             for ln in head_lanes]
    qs = [(q[:, ln] * (HD_A ** -0.5)).astype(BF16) for ln in head_lanes]

    def attend(h, k_start, bias, keep, carry):
        m, l, acc = carry
        kh = k_ref[0, pl.ds(k_start, blk_rows), head_lanes[h]]
        vh = v_ref[0, pl.ds(k_start, blk_rows), head_lanes[h]]
        s = lax.dot_general(kh, qs[h], NT_DIMS, preferred_element_type=F32) + bias
        s = jnp.where(keep, s, NEG_INF)
        m_new = jnp.maximum(m, jnp.max(s, axis=0, keepdims=True))
        alpha = jnp.exp(m - m_new)
        p = jnp.exp(s - m_new)
        l = alpha * l + jnp.sum(p, axis=0, keepdims=True)
        acc = alpha * acc + lax.dot_general(vh, p.astype(BF16), TN_DIMS, preferred_element_type=F32)
        return m_new, l, acc

    carries = []
    for h in range(H_A):
        s0 = lax.dot_general(k_ref[0, pl.ds(own_start, blk_rows), head_lanes[h]], qs[h], NT_DIMS,
                             preferred_element_type=F32) + bias_own[h]
        s0 = jnp.where(causal, s0, NEG_INF)
        m0 = jnp.max(s0, axis=0, keepdims=True)
        p0 = jnp.exp(s0 - m0)
        carry = (m0, jnp.sum(p0, axis=0, keepdims=True),
                 lax.dot_general(v_ref[0, pl.ds(own_start, blk_rows), head_lanes[h]], p0.astype(BF16),
                                 TN_DIMS, preferred_element_type=F32))
        carries.append(attend(h, prev_start, bias_prev[h], _picked(picks[h], prev_blk), carry))

    def far_body(j, cs):
        k_start = pl.multiple_of(j * blk_rows, blk_rows)
        return tuple(attend(h, k_start, relb_ref[N_BUCKETS - 1, h], _picked(picks[h], j), cs[h])
                     for h in range(H_A))

    carries = lax.fori_loop(0, jnp.maximum(kb - 1, 0), far_body, tuple(carries))
    out_t = jnp.concatenate([acc / l for (_, l, acc) in carries], axis=0)
    o_ref[0] = out_t.T.astype(o_ref.dtype)


def moba_prompt(q, kbf, vbf, kmean, rel_bias):
    n, s, _ = q.shape
    nb = s // MOBA_BLOCK
    i = jnp.arange(MOBA_BLOCK)
    rel_own = i[None, :] - i[:, None]
    bkt_own = t5_bucket(rel_own).astype(jnp.int32)
    bkt_prev = t5_bucket(rel_own + MOBA_BLOCK).astype(jnp.int32)
    grid_spec = pltpu.PrefetchScalarGridSpec(
        num_scalar_prefetch=1,
        grid=(n, nb),
        in_specs=[pl.BlockSpec((1, MOBA_BLOCK, ATT_W), lambda a, b, *_: (a, b, 0)),
                  pl.BlockSpec((1, s, ATT_W), lambda a, b, *_: (a, 0, 0)),
                  pl.BlockSpec((1, s, ATT_W), lambda a, b, *_: (a, 0, 0)),
                  pl.BlockSpec((1, nb, ATT_W), lambda a, b, *_: (a, 0, 0)),
                  pl.BlockSpec((MOBA_BLOCK, MOBA_BLOCK), lambda a, b, *_: (0, 0)),
                  pl.BlockSpec((MOBA_BLOCK, MOBA_BLOCK), lambda a, b, *_: (0, 0))],
        out_specs=pl.BlockSpec((1, MOBA_BLOCK, ATT_W), lambda a, b, *_: (a, b, 0)),
        scratch_shapes=[pltpu.VMEM((H_A, MOBA_BLOCK, MOBA_BLOCK), F32),
                        pltpu.VMEM((H_A, MOBA_BLOCK, MOBA_BLOCK), F32)],
    )
    return pl.pallas_call(
        _moba_prompt_kernel,
        grid_spec=grid_spec,
        out_shape=jax.ShapeDtypeStruct((n, s, ATT_W), BF16),
        compiler_params=_params("arbitrary", "arbitrary"),
        name="moba_prompt",
    )(rel_bias, q, kbf, vbf, kmean, bkt_own, bkt_prev)


def _moba_sample_kernel(pt_ref, q_ref, knew_ref, vnew_ref, relb_rows_ref, bkt_last_ref, bkt_own_ref,
                        ck_hbm, cv_hbm, o_ref,
                        kbuf, vbuf, kown, vown, logit, prob, means, bias_last, bias_own, sems):
    seq = pl.program_id(0)
    n_seq = pl.num_programs(0)
    n_pages = kbuf.shape[0]
    n_blocks = n_pages // PAGES_PER_BLOCK
    t_new = q_ref.shape[1]
    n_rows = t_new * H_A

    def page_copy(cache, buf, sem_slot, s, j):
        return pltpu.make_async_copy(cache.at[pt_ref[s, j]], buf.at[j], sems.at[sem_slot])

    def start_pages(cache, buf, sem_slot, s):
        for j in range(n_pages):
            page_copy(cache, buf, sem_slot, s, j).start()

    def wait_pages(cache, buf, sem_slot, s):
        for j in range(n_pages):
            page_copy(cache, buf, sem_slot, s, j).wait()

    @pl.when(seq == 0)
    def _():
        start_pages(ck_hbm, kbuf, 0, 0)
        start_pages(cv_hbm, vbuf, 1, 0)
        kown[...] = jnp.zeros_like(kown)
        vown[...] = jnp.zeros_like(vown)
        means[...] = jnp.zeros_like(means)
        relb_rows = relb_rows_ref[...]
        bias_last[...] = _bias_rows_from_buckets(bkt_last_ref[...], relb_rows)
        bias_own[...] = _bias_rows_from_buckets(bkt_own_ref[...], relb_rows)

    q = q_ref[0]
    head_mask = (lax.broadcasted_iota(jnp.int32, (H_A, ATT_W), 1) // HD_A
                 == lax.broadcasted_iota(jnp.int32, (H_A, ATT_W), 0))
    qbd = jnp.concatenate([jnp.where(head_mask, jnp.broadcast_to(q[t:t + 1], (H_A, ATT_W)), 0.0)
                           for t in range(t_new)], axis=0)
    qs = (qbd * (HD_A ** -0.5)).astype(BF16)
    kown[0:t_new, :] = knew_ref[0]
    vown[0:t_new, :] = vnew_ref[0]

    wait_pages(ck_hbm, kbuf, 0, seq)

    blk_lane = lax.broadcasted_iota(jnp.int32, means.shape, 1)

    def k_block(b, carry):
        tot = jnp.zeros((ATT_W, PAGE_SIZE), F32)
        for pg in range(PAGES_PER_BLOCK):
            j = b * PAGES_PER_BLOCK + pg
            kp = kbuf[j]
            tot = tot + kp
            logit[j] = jnp.dot(qs, kp.astype(BF16), preferred_element_type=F32)
        mean_col = jnp.sum(tot, axis=1, keepdims=True) * (1.0 / MOBA_BLOCK)
        means[...] = jnp.where(blk_lane == b, mean_col, means[...])
        return carry

    lax.fori_loop(0, n_blocks, k_block, 0, unroll=2)

    @pl.when(seq + 1 < n_seq)
    def _():
        start_pages(ck_hbm, kbuf, 0, seq + 1)

    scores = jnp.dot(qbd, means[:, 0:n_blocks], precision=HIGHEST, preferred_element_type=F32)
    picks = _top_block_ids(scores, n_blocks, min(MOBA_TOPK, n_blocks))

    far_bias = relb_rows_ref[:, N_BUCKETS - 1:N_BUCKETS]
    own_ok = bkt_own_ref[...] >= 0
    s_own = lax.dot_general(qs, kown[...].astype(BF16), NT_DIMS, preferred_element_type=F32)
    s_own = jnp.where(own_ok, s_own + bias_own[...], NEG_INF)
    mx = s_own
    for j in range(n_pages):
        b, pg = divmod(j, PAGES_PER_BLOCK)
        if b == n_blocks - 1:
            bias = bias_last[:, pg * PAGE_SIZE:(pg + 1) * PAGE_SIZE]
        else:
            bias = far_bias
        s = jnp.where(_picked(picks, b), logit[j] + bias, NEG_INF)
        logit[j] = s
        mx = jnp.maximum(mx, s)
    m = jnp.max(mx, axis=1, keepdims=True)
    p_own = jnp.exp(s_own - m)
    lsum = p_own
    for j in range(n_pages):
        p = jnp.exp(logit[j] - m)
        prob[j] = p.astype(BF16)
        lsum = lsum + p
    l = jnp.sum(lsum, axis=1, keepdims=True)

    wait_pages(cv_hbm, vbuf, 1, seq)

    def v_block(b, acc):
        for pg in range(PAGES_PER_BLOCK):
            j = b * PAGES_PER_BLOCK + pg
            acc = acc + lax.dot_general(prob[j], vbuf[j].astype(BF16), NT_DIMS,
                                        preferred_element_type=F32)
        return acc

    acc = lax.fori_loop(0, n_blocks, v_block, jnp.zeros((n_rows, ATT_W), F32), unroll=4)

    @pl.when(seq + 1 < n_seq)
    def _():
        start_pages(cv_hbm, vbuf, 1, seq + 1)

    acc = acc + jnp.dot(p_own.astype(BF16), vown[...].astype(BF16), preferred_element_type=F32)
    acc = acc / l
    for t in range(t_new):
        rows = acc[t * H_A:(t + 1) * H_A]
        o_ref[0, t:t + 1, :] = jnp.sum(jnp.where(head_mask, rows, 0.0), axis=0, keepdims=True)


def moba_sample(q, k_new, v_new, cache_k, cache_v, page_table, rel_bias):
    n, t, _ = q.shape
    n_pages = page_table.shape[1]
    assert n_pages % PAGES_PER_BLOCK == 0 and PAGES_PER_BLOCK <= n_pages <= PAGE_SIZE * PAGES_PER_BLOCK
    n_rows = t * H_A
    t_of_row = jnp.arange(n_rows) // H_A
    relb_rows = jnp.tile(rel_bias.T, (t, 1))
    bkt_last = t5_bucket(MOBA_BLOCK + t_of_row[:, None] - jnp.arange(MOBA_BLOCK)[None, :]).astype(jnp.int32)
    cols = jnp.arange(PAGE_SIZE)[None, :]
    rel_own = t_of_row[:, None] - cols
    bkt_own = jnp.where((rel_own >= 0) & (cols < t), t5_bucket(rel_own), -1).astype(jnp.int32)
    seq_spec = pl.BlockSpec((1, t, ATT_W), lambda s, *_: (s, 0, 0))
    const = lambda shape: pl.BlockSpec(shape, lambda s, *_: (0,) * len(shape))
    grid_spec = pltpu.PrefetchScalarGridSpec(
        num_scalar_prefetch=1,
        grid=(n,),
        in_specs=[seq_spec, seq_spec, seq_spec,
                  const((n_rows, N_BUCKETS)), const((n_rows, MOBA_BLOCK)), const((n_rows, PAGE_SIZE)),
                  pl.BlockSpec(memory_space=pl.ANY), pl.BlockSpec(memory_space=pl.ANY)],
        out_specs=seq_spec,
        scratch_shapes=[pltpu.VMEM((n_pages, ATT_W, PAGE_SIZE), F32),
                        pltpu.VMEM((n_pages, ATT_W, PAGE_SIZE), F32),
                        pltpu.VMEM((PAGE_SIZE, ATT_W), F32),
                        pltpu.VMEM((PAGE_SIZE, ATT_W), F32),
                        pltpu.VMEM((n_pages, n_rows, PAGE_SIZE), F32),
                        pltpu.VMEM((n_pages, n_rows, PAGE_SIZE), BF16),
                        pltpu.VMEM((ATT_W, PAGE_SIZE), F32),
                        pltpu.VMEM((n_rows, MOBA_BLOCK), F32),
                        pltpu.VMEM((n_rows, PAGE_SIZE), F32),
                        pltpu.SemaphoreType.DMA((2,))],
    )
    return pl.pallas_call(
        _moba_sample_kernel,
        grid_spec=grid_spec,
        out_shape=jax.ShapeDtypeStruct((n, t, ATT_W), F32),
        compiler_params=_params("arbitrary"),
        name="moba_sample",
    )(page_table, q, k_new, v_new, relb_rows, bkt_last, bkt_own, cache_k, cache_v)


def _swap_pairs(x):
    lane = lax.broadcasted_iota(jnp.int32, x.shape, 1)
    width = x.shape[1]
    return jnp.where(lane % 2 == 0, pltpu.roll(x, width - 1, 1), pltpu.roll(x, 1, 1))


def _retention_kernel(q_ref, k_ref, v_ref, g_ref, cos_ref, sin_ref, decay_ref, cross_ref, kw_ref,
                      sdec_ref, ng_ref, nb_ref, st_in_ref, y_ref, st_out_ref, st):
    c = pl.program_id(1)

    @pl.when(c == 0)
    def _():
        st[...] = st_in_ref[0]

    cos = cos_ref[...]
    sin = sin_ref[...]
    q = q_ref[0]
    k = k_ref[0]
    v = v_ref[0]
    g = g_ref[0]
    for h in range(H_R):
        lanes = slice(h * DK_R, (h + 1) * DK_R)
        qh = q[:, lanes]
        kh = k[:, lanes]
        qh = (qh * cos + _swap_pairs(qh) * sin).astype(BF16)
        kh = (kh * cos + _swap_pairs(kh) * sin) * (DK_R ** -0.5)
        vh = v[:, lanes]
        state = st[h]
        scores = lax.dot_general(qh, kh.astype(BF16), NT_DIMS, preferred_element_type=F32) * decay_ref[h]
        inner = jnp.dot(scores.astype(BF16), vh, preferred_element_type=F32)
        cross = jnp.dot(qh, state.astype(BF16), preferred_element_type=F32) * cross_ref[:, lanes]
        kd = (kh * kw_ref[:, lanes]).astype(BF16)
        st[h] = sdec_ref[h] * state + lax.dot_general(kd, vh, TN_DIMS, preferred_element_type=F32)
        o = inner + cross
        o = _layer_norm(o) * ng_ref[:, lanes] + nb_ref[:, lanes]
        y_ref[0, :, lanes] = (_silu(g[:, lanes]) * o).astype(y_ref.dtype)

    @pl.when(c == pl.num_programs(1) - 1)
    def _():
        st_out_ref[0] = st[...]


def _retention_tables(pos, n_valid, chunk):
    theta = 1.0 / (ROPE_BASE ** jnp.linspace(0.0, 1.0, DK_R // 2, dtype=F32))
    ang = pos.astype(F32)[:, None] * theta[None, :]
    cos = jnp.repeat(jnp.cos(ang), 2, axis=1)
    sin = jnp.repeat(jnp.sin(ang), 2, axis=1) * jnp.tile(jnp.array([-1.0, 1.0], F32), DK_R // 2)[None, :]
    log_g = jnp.log1p(-jnp.exp2(-5.0 - jnp.arange(H_R, dtype=F32)))
    i = jnp.arange(chunk, dtype=F32)
    valid = i < n_valid
    diff = i[:, None] - i[None, :]
    decay = jnp.where(diff >= 0, jnp.exp(jnp.maximum(diff, 0.0)[None] * log_g[:, None, None]), 0.0)
    decay = jnp.where(valid[None, :, None] & valid[None, None, :], decay, 0.0)
    cross = jnp.where(valid[:, None], jnp.exp((i + 1.0)[:, None] * log_g[None, :]), 0.0)
    kw = jnp.where(valid[:, None], jnp.exp((n_valid - 1.0 - i)[:, None] * log_g[None, :]), 0.0)
    sdec = jnp.exp(n_valid * log_g)
    return dict(cos=cos, sin=sin, decay=decay,
                cross=jnp.repeat(cross, DV_R, axis=1), kw=jnp.repeat(kw, DK_R, axis=1),
                sdec=jnp.broadcast_to(sdec[:, None, None], (H_R, 1, DV_R)))


def retention(q, k, v, g, state0, tables, norm_g, norm_b):
    n, s, _ = q.shape
    chunk = tables["decay"].shape[1]
    n_chunks = s // chunk
    seq = pl.BlockSpec((1, chunk, RET_QK_W), lambda a, c: (a, c, 0))
    per_chunk = pl.BlockSpec((chunk, DK_R), lambda a, c: (c, 0))
    per_row = pl.BlockSpec((chunk, RET_QK_W), lambda a, c: (0, 0))
    st_spec = pl.BlockSpec((1, H_R, DK_R, DV_R), lambda a, c: (a, 0, 0, 0))
    vec = pl.BlockSpec((1, RET_V_W), lambda a, c: (0, 0))
    return pl.pallas_call(
        _retention_kernel,
        grid=(n, n_chunks),
        in_specs=[seq, seq, seq, seq, per_chunk, per_chunk,
                  pl.BlockSpec((H_R, chunk, chunk), lambda a, c: (0, 0, 0)),
                  per_row, per_row,
                  pl.BlockSpec((H_R, 1, DV_R), lambda a, c: (0, 0, 0)),
                  vec, vec, st_spec],
        out_specs=[seq, st_spec],
        out_shape=[jax.ShapeDtypeStruct((n, s, RET_V_W), BF16),
                   jax.ShapeDtypeStruct((n, H_R, DK_R, DV_R), F32)],
        scratch_shapes=[pltpu.VMEM((H_R, DK_R, DV_R), F32)],
        compiler_params=_params("arbitrary", "arbitrary"),
        name="retention",
    )(q, k, v, g, tables["cos"], tables["sin"], tables["decay"], tables["cross"], tables["kw"],
      tables["sdec"], norm_g.reshape(1, -1), norm_b.reshape(1, -1), state0)


def _mix_kernel(att_ref, ret_ref, ga_ref, gb_ref, x_ref, gate1_ref, shift2_ref, scale2_ref,
                wa_ref, wb_ref, wo_ref, g1_ref, b1_ref, wr_ref, x1_ref, h2_ref, h2w_ref, logit_ref):
    ya = jnp.dot(att_ref[...], wa_ref[...], preferred_element_type=F32)
    yb = jnp.dot(ret_ref[...], wb_ref[...], preferred_element_type=F32)
    mixed_in = jax.nn.sigmoid(ga_ref[...]) * ya + jax.nn.sigmoid(gb_ref[...]) * yb
    mixed = jnp.dot(mixed_in.astype(BF16), wo_ref[...], preferred_element_type=F32)
    x1 = _layer_norm(ALPHA * x_ref[...] + gate1_ref[0] * mixed) * g1_ref[...] + b1_ref[...]
    x1_ref[...] = x1
    h2 = _layer_norm(x1) * (1.0 + scale2_ref[0]) + shift2_ref[0]
    h2_bf = h2.astype(BF16)
    h2_ref[...] = h2_bf
    h2w_ref[...] = _pack_bf16_halves(h2_bf)
    logit_ref[...] = jnp.dot(h2, wr_ref[...], precision=HIGHEST, preferred_element_type=F32)


def mix_out(att, ret, ga, gb, x, gate1, shift2, scale2, wa, wb, wo, ln_g, ln_b, w_router, tiles_per_group):
    m = x.shape[0]
    tm = ROW_TILE
    r = gate1.shape[1]
    row = lambda w: pl.BlockSpec((tm, w), lambda i: (i, 0))
    mod = pl.BlockSpec((1, r, D_MODEL), lambda i: (i // tiles_per_group, 0, 0))
    return pl.pallas_call(
        _mix_kernel,
        grid=(m // tm,),
        in_specs=[row(ATT_W), row(RET_V_W), row(D_MODEL), row(D_MODEL), row(D_MODEL), mod, mod, mod,
                  _resident((ATT_W, D_MODEL)), _resident((RET_V_W, D_MODEL)), _resident((D_MODEL, D_MODEL)),
                  _resident((1, D_MODEL)), _resident((1, D_MODEL)), _resident((D_MODEL, N_EXPERTS))],
        out_specs=[row(D_MODEL), row(D_MODEL), row(D_MODEL // 2), row(N_EXPERTS)],
        out_shape=[jax.ShapeDtypeStruct((m, D_MODEL), F32), jax.ShapeDtypeStruct((m, D_MODEL), BF16),
                   jax.ShapeDtypeStruct((m, D_MODEL // 2), jnp.uint32),
                   jax.ShapeDtypeStruct((m, N_EXPERTS), F32)],
        compiler_params=_params("arbitrary"),
        name="mix_out",
    )(att, ret, ga, gb, x, gate1, shift2, scale2, wa, wb, wo, ln_g.reshape(1, -1), ln_b.reshape(1, -1),
      w_router)


def _first_max(x, ids, sentinel):
    best = jnp.max(x, axis=0, keepdims=True)
    first = jnp.min(jnp.where(x == best, ids, sentinel), axis=0, keepdims=True)
    return best, first


def _route_kernel(logit_ref, bias_ref, idx_ref, gate_ref):
    s = jax.nn.sigmoid(logit_ref[...].T)
    sb = s + bias_ref[...]
    tm = s.shape[1]
    gid = lax.broadcasted_iota(jnp.int32, (GROUP_SIZE, tm), 0)
    grp_scores = []
    for gi in range(N_GROUPS):
        grp = sb[gi * GROUP_SIZE:(gi + 1) * GROUP_SIZE]
        top1, first = _first_max(grp, gid, GROUP_SIZE)
        top2 = jnp.max(jnp.where(gid == first, NEG_INF, grp), axis=0, keepdims=True)
        grp_scores.append(top1 + top2)
    gs = jnp.concatenate(grp_scores, axis=0)
    gids = lax.broadcasted_iota(jnp.int32, (N_GROUPS, tm), 0)
    gmask = jnp.zeros((N_GROUPS, tm), jnp.bool_)
    for _ in range(TOPK_GROUPS):
        _, first = _first_max(gs, gids, N_GROUPS)
        hit = gids == first
        gmask = jnp.logical_or(gmask, hit)
        gs = jnp.where(hit, NEG_INF, gs)
    masked = jnp.concatenate(
        [jnp.where(gmask[gi:gi + 1], sb[gi * GROUP_SIZE:(gi + 1) * GROUP_SIZE], NEG_INF)
         for gi in range(N_GROUPS)], axis=0)
    eids = lax.broadcasted_iota(jnp.int32, (N_EXPERTS, tm), 0)
    picks, weights = [], []
    for _ in range(TOP_K):
        _, first = _first_max(masked, eids, N_EXPERTS)
        hit = eids == first
        picks.append(first)
        weights.append(jnp.sum(jnp.where(hit, s, 0.0), axis=0, keepdims=True))
        masked = jnp.where(hit, NEG_INF, masked)
    w = jnp.concatenate(weights, axis=0)
    idx_ref[...] = jnp.concatenate(picks, axis=0)
    gate_ref[...] = w / jnp.sum(w, axis=0, keepdims=True) * ROUTED_SCALE


def route(logits, router_bias):
    m = logits.shape[0]
    tm = ROW_TILE
    return pl.pallas_call(
        _route_kernel,
        grid=(m // tm,),
        in_specs=[pl.BlockSpec((tm, N_EXPERTS), lambda i: (i, 0)),
                  pl.BlockSpec((N_EXPERTS, 1), lambda i: (0, 0))],
        out_specs=[pl.BlockSpec((TOP_K, tm), lambda i: (0, i)), pl.BlockSpec((TOP_K, tm), lambda i: (0, i))],
        out_shape=[jax.ShapeDtypeStruct((TOP_K, m), jnp.int32), jax.ShapeDtypeStruct((TOP_K, m), F32)],
        compiler_params=_params("arbitrary"),
        name="route",
    )(logits, router_bias.reshape(-1, 1))


def _expert_kernel(blk_e_ref, n_used_ref, blk_pos_ref, tok_a_ref, tok_b_ref, wg_ref, wu_ref, wd_ref, hw_ref,
                   y_ref, xw):
    i = pl.program_id(0)
    n_used = n_used_ref[0]
    tm = xw.shape[0]

    @pl.when(i < n_used)
    def _():
        off = blk_pos_ref[i] & (tm - 1)
        for r in range(tm):
            pos = off + r
            tok = jnp.where(pos < tm, tok_a_ref[0, 0, jnp.minimum(pos, tm - 1)],
                            tok_b_ref[0, 0, jnp.maximum(pos - tm, 0)])
            xw[r:r + 1, :] = hw_ref[pl.ds(tok, 1), :]
        x = _unpack_bf16_halves(xw[...])
        a = (_silu(jnp.dot(x, wg_ref[0].astype(BF16), preferred_element_type=F32))
             * jnp.dot(x, wu_ref[0].astype(BF16), preferred_element_type=F32))
        y_ref[...] = jnp.dot(a.astype(BF16), wd_ref[0].astype(BF16), preferred_element_type=F32)

    @pl.when(i >= n_used)
    def _():
        y_ref[...] = jnp.zeros_like(y_ref)


def routed_experts(h_words, tok_windows, blk_pos, blk_e, n_used, w_gate, w_up, w_down):
    n_blocks = blk_e.shape[0]
    tm = tok_windows.shape[2]
    assert tm & (tm - 1) == 0
    m, half = h_words.shape
    weights = lambda shape: pl.BlockSpec(shape, lambda i, be, nu, bp: (be[i], 0, 0))
    grid_spec = pltpu.PrefetchScalarGridSpec(
        num_scalar_prefetch=3,
        grid=(n_blocks,),
        in_specs=[pl.BlockSpec((1, 1, tm), lambda i, be, nu, bp: (bp[i] // tm, 0, 0),
                               memory_space=pltpu.SMEM),
                  pl.BlockSpec((1, 1, tm), lambda i, be, nu, bp: (bp[i] // tm + 1, 0, 0),
                               memory_space=pltpu.SMEM),
                  weights((1, D_MODEL, D_EXPERT)), weights((1, D_MODEL, D_EXPERT)),
                  weights((1, D_EXPERT, D_MODEL)),
                  _resident((m, half))],
        out_specs=pl.BlockSpec((tm, D_MODEL), lambda i, be, nu, bp: (i, 0)),
        scratch_shapes=[pltpu.VMEM((tm, half), jnp.uint32)],
    )
    return pl.pallas_call(
        _expert_kernel,
        grid_spec=grid_spec,
        out_shape=jax.ShapeDtypeStruct((n_blocks * tm, D_MODEL), F32),
        compiler_params=_params("arbitrary"),
        name="routed_experts",
    )(blk_e, n_used, blk_pos, tok_windows, tok_windows, w_gate, w_up, w_down, h_words)


def _final_kernel(dest_ref, dest_next_ref, x1_ref, h2_ref, gate_ref, gate2_ref, wsg_ref, wsu_ref, wsd_ref,
                  g2_ref, b2_ref, ys_hbm, y_ref, rbuf, sems):
    i = pl.program_id(0)
    n_steps = pl.num_programs(0)
    tm = x1_ref.shape[0]
    n_gather = TOP_K * tm
    slot = i % 2

    def row_copy(dest_smem, r, to_slot):
        return pltpu.make_async_copy(ys_hbm.at[pl.ds(dest_smem[0, 0, r], 1)],
                                     rbuf.at[to_slot, pl.ds(r, 1)], sems.at[to_slot])

    def start_rows(dest_smem, to_slot):
        unroll = 8

        def body(g, c):
            for u in range(unroll):
                row_copy(dest_smem, g * unroll + u, to_slot).start(priority=u % 2)
            return c
        lax.fori_loop(0, n_gather // unroll, body, 0)

    def wait_rows(to_slot):
        pltpu.make_async_copy(ys_hbm.at[pl.ds(0, n_gather)], rbuf.at[to_slot], sems.at[to_slot]).wait()

    @pl.when(i == 0)
    def _():
        start_rows(dest_ref, 0)

    @pl.when(i + 1 < n_steps)
    def _():
        start_rows(dest_next_ref, 1 - slot)

    h = h2_ref[...]
    a = (_silu(jnp.dot(h, wsg_ref[...], preferred_element_type=F32))
         * jnp.dot(h, wsu_ref[...], preferred_element_type=F32))
    f = jnp.dot(a.astype(BF16), wsd_ref[...], preferred_element_type=F32)
    wait_rows(slot)
    gate = gate_ref[...]
    routed = jnp.zeros((tm, D_MODEL), F32)
    for k in range(TOP_K):
        routed = routed + rbuf[slot, k * tm:(k + 1) * tm, :] * gate[:, k:k + 1]
    f = routed + f
    y_ref[...] = _layer_norm(ALPHA * x1_ref[...] + gate2_ref[...] * f) * g2_ref[...] + b2_ref[...]


def final_out(x1, h2, gate, gate2_rows, dest, ys, wsg, wsu, wsd, ln_g, ln_b):
    m = x1.shape[0]
    tm = COMBINE_TILE
    n_steps = m // tm
    row = lambda w: pl.BlockSpec((tm, w), lambda i: (i, 0))
    dest_spec = lambda f: pl.BlockSpec((1, 1, TOP_K * tm), lambda i: (f(i), 0, 0), memory_space=pltpu.SMEM)
    return pl.pallas_call(
        _final_kernel,
        grid=(n_steps,),
        in_specs=[dest_spec(lambda i: i), dest_spec(lambda i: jnp.minimum(i + 1, n_steps - 1)),
                  row(D_MODEL), row(D_MODEL), row(TOP_K), row(D_MODEL),
                  _resident((D_MODEL, D_SHARED)), _resident((D_MODEL, D_SHARED)), _resident((D_SHARED, D_MODEL)),
                  _resident((1, D_MODEL)), _resident((1, D_MODEL)),
                  pl.BlockSpec(memory_space=pl.ANY)],
        out_specs=row(D_MODEL),
        out_shape=jax.ShapeDtypeStruct((m, D_MODEL), F32),
        scratch_shapes=[pltpu.VMEM((2, TOP_K * tm, D_MODEL), F32), pltpu.SemaphoreType.DMA((2,))],
        compiler_params=_params("arbitrary"),
        name="final_out",
    )(dest, dest, x1, h2, gate, gate2_rows, wsg, wsu, wsd, ln_g.reshape(1, -1), ln_b.reshape(1, -1), ys)


def _dispatch(idx_t):
    m = idx_t.shape[1]
    bm = MOE_ROW_TILE
    n_pairs = m * TOP_K
    experts = jnp.arange(N_EXPERTS, dtype=jnp.int32)
    pair_ids = jnp.arange(n_pairs, dtype=jnp.int32)
    flat_e = idx_t.T.reshape(-1)
    se, order = lax.sort((flat_e, pair_ids), num_keys=1, is_stable=True)
    counts = jnp.sum((flat_e[:, None] == experts[None, :]).astype(jnp.int32), axis=0)
    padded = (counts + bm - 1) // bm * bm
    pad_end = jnp.cumsum(padded)
    pad_start = pad_end - padded
    start = jnp.cumsum(counts) - counts
    shift = (pad_start - start).astype(jnp.int32)
    dest_sorted = pair_ids + jnp.sum(jnp.where(se[:, None] == experts[None, :], shift[None, :], 0), axis=1)
    n_blocks = -(-(n_pairs + N_EXPERTS * (bm - 1)) // bm)
    _, dest = lax.sort((order, dest_sorted), num_keys=1)
    blk_row0 = jnp.arange(n_blocks, dtype=jnp.int32) * bm
    blk_e = jnp.minimum(jnp.searchsorted(pad_end, blk_row0, side='right'), N_EXPERTS - 1).astype(jnp.int32)
    n_used = (pad_end[-1] // bm).astype(jnp.int32).reshape(1)
    blk_pos = jnp.where(jnp.arange(n_blocks) < n_used[0], blk_row0 - shift[blk_e], 0).astype(jnp.int32)
    n_win = -(-n_pairs // bm) + 1
    tok_windows = jnp.pad(order // TOP_K, (0, n_win * bm - n_pairs)).reshape(n_win, 1, bm)
    tc = COMBINE_TILE
    dest_tiles = dest.reshape(m // tc, tc, TOP_K).transpose(0, 2, 1).reshape(m // tc, 1, TOP_K * tc)
    return tok_windows, blk_pos, blk_e, n_used, dest_tiles


def _expand_rows(v, reps):
    return jnp.repeat(v, reps, axis=0).reshape(-1, ROW_TILE, v.shape[-1])


def kernel(x_prompt, x_sample, cache_k, cache_v, state_ret, page_table, c_prompt, c_sample, w_ada, b_ada, w_in, rel_bias, ret_norm_g, ret_norm_b, w_branch_a, w_branch_b, w_out, ln1_g, ln1_b, w_router, router_bias, w_gate, w_up, w_down, w_shared_gate, w_shared_up, w_shared_down, ln2_g, ln2_b):
    assert DEPTH == 1
    l = 0
    nb, s, d = x_prompt.shape
    ns, t, _ = x_sample.shape
    past_len = page_table.shape[1] * PAGE_SIZE
    mp, ms = nb * s, ns * t

    c_all = jnp.concatenate([c_prompt, c_sample], axis=0)
    n_c = c_all.shape[0]
    c_all = jnp.pad(c_all, ((0, -n_c % 8), (0, 0)))
    mod = ada_modulation(c_all, w_ada[l], b_ada[l])
    mods_p = [mod[:nb, i * d:(i + 1) * d].reshape(nb, 1, d) for i in range(6)]
    mods_s = [_expand_rows(mod[nb:nb + ns, i * d:(i + 1) * d], t) for i in range(6)]
    tiles_p = s // ROW_TILE

    w_in_bf = w_in[l].astype(BF16)
    pp = in_proj(x_prompt.reshape(mp, d), mods_p[0], mods_p[1], w_in_bf, tiles_p)
    ps = in_proj(x_sample.reshape(ms, d), mods_s[0], mods_s[1], w_in_bf, 1)

    att_p = moba_prompt(pp["qa"].reshape(nb, s, ATT_W), pp["kbf"].reshape(nb, s, ATT_W),
                        pp["vbf"].reshape(nb, s, ATT_W), pp["kmean"].reshape(nb, s // MOBA_BLOCK, ATT_W),
                        rel_bias)
    tab_p = _retention_tables(jnp.arange(s), float(RET_CHUNK), RET_CHUNK)
    ret_p, st_p = retention(pp["qr"].reshape(nb, s, -1), pp["kr"].reshape(nb, s, -1),
                            pp["vr"].reshape(nb, s, -1), pp["gr"].reshape(nb, s, -1),
                            jnp.zeros((nb, H_R, DK_R, DV_R), F32), tab_p, ret_norm_g[l], ret_norm_b[l])

    n_pool = cache_k.shape[1]
    page_major = lambda c: jnp.transpose(c[l], (0, 2, 3, 1)).reshape(n_pool, ATT_W, PAGE_SIZE)
    att_s = moba_sample(ps["qa"].reshape(ns, t, ATT_W), ps["ka"].reshape(ns, t, ATT_W),
                        ps["va"].reshape(ns, t, ATT_W), page_major(cache_k), page_major(cache_v),
                        page_table, rel_bias)
    pad_t = lambda a: jnp.pad(a.reshape(ns, t, -1), ((0, 0), (0, RET_CHUNK - t), (0, 0)))
    pos_s = past_len + jnp.arange(RET_CHUNK)
    tab_s = _retention_tables(pos_s, float(t), RET_CHUNK)
    ret_s, st_s = retention(pad_t(ps["qr"]), pad_t(ps["kr"]), pad_t(ps["vr"]), pad_t(ps["gr"]),
                            state_ret[l], tab_s, ret_norm_g[l], ret_norm_b[l])
    ret_s = ret_s[:, :t].reshape(ms, RET_V_W)

    wa, wb, wo = w_branch_a[l].astype(BF16), w_branch_b[l].astype(BF16), w_out[l].astype(BF16)
    x1_p, h2_p, hw_p, lg_p = mix_out(att_p.reshape(mp, ATT_W), ret_p.reshape(mp, RET_V_W), pp["ga"],
                                     pp["gb"], x_prompt.reshape(mp, d), mods_p[2], mods_p[3], mods_p[4],
                                     wa, wb, wo, ln1_g[l], ln1_b[l], w_router[l], tiles_p)
    x1_s, h2_s, hw_s, lg_s = mix_out(att_s.reshape(ms, ATT_W).astype(BF16), ret_s, ps["ga"], ps["gb"],
                                     x_sample.reshape(ms, d), mods_s[2], mods_s[3], mods_s[4],
                                     wa, wb, wo, ln1_g[l], ln1_b[l], w_router[l], 1)

    x1 = jnp.concatenate([x1_p, x1_s], axis=0)
    h2 = jnp.concatenate([h2_p, h2_s], axis=0)
    idx_t, gate_t = route(jnp.concatenate([lg_p, lg_s], axis=0), router_bias[l])
    tok_windows, blk_pos, blk_e, n_used, dest_tiles = _dispatch(idx_t)
    ys = routed_experts(jnp.concatenate([hw_p, hw_s], axis=0), tok_windows, blk_pos, blk_e, n_used,
                        w_gate[l], w_up[l], w_down[l])
    gate2_rows = jnp.concatenate([jnp.broadcast_to(mods_p[5], (nb, s, d)).reshape(mp, d),
                                  mods_s[5].reshape(ms, d)], axis=0)
    wsg, wsu, wsd = (w_shared_gate[l].astype(BF16), w_shared_up[l].astype(BF16),
                     w_shared_down[l].astype(BF16))
    y = final_out(x1, h2, gate_t.T, gate2_rows, dest_tiles, ys, wsg, wsu, wsd, ln2_g[l], ln2_b[l])

    return (y[:mp].reshape(nb, s, d), y[mp:].reshape(ns, t, d),
            pp["ka"].reshape(1, nb, s, H_A, HD_A), pp["va"].reshape(1, nb, s, H_A, HD_A), st_p[None],
            ps["ka"].reshape(1, ns, t, H_A, HD_A), ps["va"].reshape(1, ns, t, H_A, HD_A), st_s[None])
```

```python
import math

import jax
import jax.numpy as jnp
from jax import lax
from jax.experimental import pallas as pl
from jax.experimental.pallas import tpu as pltpu

D_MODEL = 1024
DEPTH = 1
PAGE_SIZE = 128
H_A = 8
HD_A = 64
ATT_W = H_A * HD_A
MOBA_BLOCK = 256
MOBA_TOPK = 3
PAGES_PER_BLOCK = MOBA_BLOCK // PAGE_SIZE
N_BUCKETS = 32
MAX_DISTANCE = 128
H_R = 4
DK_R = 128
DV_R = 128
RET_QK_W = H_R * DK_R
RET_V_W = H_R * DV_R
RET_CHUNK = 128
ROPE_BASE = 10000.0
N_EXPERTS = 256
TOP_K = 8
N_GROUPS = 8
TOPK_GROUPS = 4
GROUP_SIZE = N_EXPERTS // N_GROUPS
D_EXPERT = 256
D_SHARED = 256
ROUTED_SCALE = 2.5
ALPHA = (2.0 * DEPTH) ** 0.25
LN_EPS = 1e-5
D_IN = 3 * ATT_W + 2 * RET_QK_W + 2 * RET_V_W + 2 * D_MODEL

F32 = jnp.float32
BF16 = jnp.bfloat16
NEG_INF = float("-inf")
HIGHEST = lax.Precision.HIGHEST
NT_DIMS = (((1,), (1,)), ((), ()))
TN_DIMS = (((0,), (0,)), ((), ()))

VMEM_LIMIT_BYTES = 56 * 1024 * 1024
ROW_TILE = 512
MOE_ROW_TILE = 256
COMBINE_TILE = 256


def _params(*semantics):
    return pltpu.CompilerParams(dimension_semantics=semantics, vmem_limit_bytes=VMEM_LIMIT_BYTES)


def _layer_norm(x):
    mu = jnp.mean(x, axis=-1, keepdims=True)
    xc = x - mu
    var = jnp.mean(xc * xc, axis=-1, keepdims=True)
    return xc * lax.rsqrt(var + LN_EPS)


def _silu(x):
    return x * jax.nn.sigmoid(x)


def _pack_bf16_halves(x):
    bits = lax.bitcast_convert_type(x.astype(F32), jnp.uint32)
    w = x.shape[1] // 2
    return bits[:, w:] | (bits[:, :w] >> 16)


def _unpack_bf16_halves(words):
    lo = lax.bitcast_convert_type(words << 16, F32)
    hi = lax.bitcast_convert_type(words & jnp.uint32(0xFFFF0000), F32)
    return jnp.concatenate([lo, hi], axis=1).astype(BF16)


def _resident(shape):
    zeros = (0,) * len(shape)
    return pl.BlockSpec(shape, lambda *_: zeros, pipeline_mode=pl.Buffered(1))


def _ada_kernel(c_ref, w_ref, b_ref, o_ref):
    a = _silu(c_ref[...])
    o_ref[...] = jnp.dot(a, w_ref[...], precision=HIGHEST, preferred_element_type=F32) + b_ref[...]


def ada_modulation(c, w_ada, b_ada):
    n = c.shape[0]
    tn = 1536
    return pl.pallas_call(
        _ada_kernel,
        grid=(6 * D_MODEL // tn,),
        in_specs=[pl.BlockSpec((n, D_MODEL), lambda j: (0, 0)),
                  pl.BlockSpec((D_MODEL, tn), lambda j: (0, j)),
                  pl.BlockSpec((1, tn), lambda j: (0, j))],
        out_specs=pl.BlockSpec((n, tn), lambda j: (0, j)),
        out_shape=jax.ShapeDtypeStruct((n, 6 * D_MODEL), F32),
        compiler_params=_params("arbitrary"),
        name="ada_modulation",
    )(c, w_ada, b_ada.reshape(1, -1))


def _in_proj_kernel(x_ref, shift_ref, scale_ref, w_ref,
                    qa_ref, ka_ref, va_ref, kbf_ref, vbf_ref, kmean_ref,
                    qr_ref, kr_ref, vr_ref, gr_ref, ga_ref, gb_ref):
    h = (_layer_norm(x_ref[...]) * (1.0 + scale_ref[0]) + shift_ref[0]).astype(BF16)

    def proj(start, width):
        return jnp.dot(h, w_ref[:, start:start + width], preferred_element_type=F32)

    qa_ref[...] = proj(0, ATT_W)
    ka = proj(ATT_W, ATT_W)
    ka_ref[...] = ka
    kbf_ref[...] = ka.astype(BF16)
    n_blk = ka.shape[0] // MOBA_BLOCK
    for b in range(n_blk):
        kmean_ref[b] = jnp.mean(ka[b * MOBA_BLOCK:(b + 1) * MOBA_BLOCK], axis=0, keepdims=True)
    va = proj(2 * ATT_W, ATT_W)
    va_ref[...] = va
    vbf_ref[...] = va.astype(BF16)
    off = 3 * ATT_W
    qr_ref[...] = proj(off, RET_QK_W)
    kr_ref[...] = proj(off + RET_QK_W, RET_QK_W)
    off += 2 * RET_QK_W
    vr_ref[...] = proj(off, RET_V_W).astype(BF16)
    gr_ref[...] = proj(off + RET_V_W, RET_V_W)
    off += 2 * RET_V_W
    ga_ref[...] = proj(off, D_MODEL)
    gb_ref[...] = proj(off + D_MODEL, D_MODEL)


def in_proj(x, shift, scale, w_in_bf16, tiles_per_group):
    m = x.shape[0]
    tm = ROW_TILE
    r = shift.shape[1]
    row = lambda w: pl.BlockSpec((tm, w), lambda i: (i, 0))
    mod = pl.BlockSpec((1, r, D_MODEL), lambda i: (i // tiles_per_group, 0, 0))
    widths = dict(qa=ATT_W, ka=ATT_W, va=ATT_W, kbf=ATT_W, vbf=ATT_W, qr=RET_QK_W, kr=RET_QK_W,
                  vr=RET_V_W, gr=RET_V_W, ga=D_MODEL, gb=D_MODEL)
    dtypes = dict(kbf=BF16, vbf=BF16, vr=BF16)
    names = ["qa", "ka", "va", "kbf", "vbf", "kmean", "qr", "kr", "vr", "gr", "ga", "gb"]
    out_specs, out_shapes = [], []
    for nm in names:
        if nm == "kmean":
            out_specs.append(pl.BlockSpec((tm // MOBA_BLOCK, 1, ATT_W), lambda i: (i, 0, 0)))
            out_shapes.append(jax.ShapeDtypeStruct((m // MOBA_BLOCK, 1, ATT_W), F32))
        else:
            out_specs.append(row(widths[nm]))
            out_shapes.append(jax.ShapeDtypeStruct((m, widths[nm]), dtypes.get(nm, F32)))
    outs = pl.pallas_call(
        _in_proj_kernel,
        grid=(m // tm,),
        in_specs=[row(D_MODEL), mod, mod, _resident((D_MODEL, D_IN))],
        out_specs=out_specs,
        out_shape=out_shapes,
        compiler_params=_params("arbitrary"),
        name="in_proj",
    )(x, shift, scale, w_in_bf16)
    return dict(zip(names, outs))


def t5_bucket(rel):
    n = jnp.maximum(rel, 0)
    max_exact = N_BUCKETS // 2
    ratio = jnp.log(jnp.maximum(n, max_exact).astype(F32) / max_exact) / math.log(MAX_DISTANCE / max_exact)
    large = jnp.minimum(max_exact + (ratio * (N_BUCKETS - max_exact)).astype(jnp.int32), N_BUCKETS - 1)
    return jnp.where(n < max_exact, n, large)


def _bias_from_buckets(bucket, rel_bias_ref, head):
    out = jnp.zeros(bucket.shape, F32)
    for b in range(N_BUCKETS):
        out = jnp.where(bucket == b, rel_bias_ref[b, head], out)
    return out


def _bias_rows_from_buckets(bucket, relb_rows):
    out = jnp.zeros(bucket.shape, F32)
    for b in range(N_BUCKETS):
        out = jnp.where(bucket == b, relb_rows[:, b:b + 1], out)
    return out


def _top_block_ids(scores, n_eligible, n_pick, axis=1):
    n_blocks = scores.shape[axis]
    blk = lax.broadcasted_iota(jnp.int32, scores.shape, axis)
    sc = jnp.where(blk < n_eligible, scores, NEG_INF)
    picks = []
    for _ in range(n_pick):
        best = jnp.max(sc, axis=axis, keepdims=True)
        first = jnp.min(jnp.where(sc == best, blk, n_blocks), axis=axis, keepdims=True)
        picks.append(jnp.where(first < n_eligible, first, -1))
        sc = jnp.where(blk == first, NEG_INF, sc)
    return picks


def _picked(picks, block_id):
    keep = picks[0] == block_id
    for p in picks[1:]:
        keep = jnp.logical_or(keep, p == block_id)
    return keep


def _moba_prompt_kernel(relb_ref, q_ref, k_ref, v_ref, kmean_ref, bkt_own_ref, bkt_prev_ref,
                        o_ref, bias_own, bias_prev):
    n_i = pl.program_id(0)
    kb = pl.program_id(1)
    blk_rows = MOBA_BLOCK

    @pl.when(jnp.logical_and(n_i == 0, kb == 0))
    def _():
        for h in range(H_A):
            bias_own[h] = _bias_from_buckets(bkt_own_ref[...], relb_ref, h)
            bias_prev[h] = _bias_from_buckets(bkt_prev_ref[...], relb_ref, h)

    q = q_ref[0]
    kmean = kmean_ref[0]
    key_i = lax.broadcasted_iota(jnp.int32, (blk_rows, blk_rows), 0)
    qry_i = lax.broadcasted_iota(jnp.int32, (blk_rows, blk_rows), 1)
    causal = qry_i >= key_i
    own_start = pl.multiple_of(kb * blk_rows, blk_rows)
    prev_blk = jnp.maximum(kb - 1, 0)
    prev_start = pl.multiple_of(prev_blk * blk_rows, blk_rows)
    head_lanes = [slice(h * HD_A, (h + 1) * HD_A) for h in range(H_A)]
    picks = [_top_block_ids(lax.dot_general(kmean[:, ln], q[:, ln], NT_DIMS, precision=HIGHEST,
                                            preferred_element_type=F32), kb, MOBA_TOPK, axis=0)
             for ln in head_lanes]
    qs = [(q[:, ln] * (HD_A ** -0.5)).astype(BF16) for ln in head_lanes]

    def attend(h, k_start, bias, keep, carry):
        m, l, acc = carry
        kh = k_ref[0, pl.ds(k_start, blk_rows), head_lanes[h]]
        vh = v_ref[0, pl.ds(k_start, blk_rows), head_lanes[h]]
        s = lax.dot_general(kh, qs[h], NT_DIMS, preferred_element_type=F32) + bias
        s = jnp.where(keep, s, NEG_INF)
        m_new = jnp.maximum(m, jnp.max(s, axis=0, keepdims=True))
        alpha = jnp.exp(m - m_new)
        p = jnp.exp(s - m_new)
        l = alpha * l + jnp.sum(p, axis=0, keepdims=True)
        acc = alpha * acc + lax.dot_general(vh, p.astype(BF16), TN_DIMS, preferred_element_type=F32)
        return m_new, l, acc

    carries = []
    for h in range(H_A):
        s0 = lax.dot_general(k_ref[0, pl.ds(own_start, blk_rows), head_lanes[h]], qs[h], NT_DIMS,
                             preferred_element_type=F32) + bias_own[h]
        s0 = jnp.where(causal, s0, NEG_INF)
        m0 = jnp.max(s0, axis=0, keepdims=True)
        p0 = jnp.exp(s0 - m0)
        carry = (m0, jnp.sum(p0, axis=0, keepdims=True),
                 lax.dot_general(v_ref[0, pl.ds(own_start, blk_rows), head_lanes[h]], p0.astype(BF16),
                                 TN_DIMS, preferred_element_type=F32))
        carries.append(attend(h, prev_start, bias_prev[h], _picked(picks[h], prev_blk), carry))

    def far_body(j, cs):
        k_start = pl.multiple_of(j * blk_rows, blk_rows)
        return tuple(attend(h, k_start, relb_ref[N_BUCKETS - 1, h], _picked(picks[h], j), cs[h])
                     for h in range(H_A))

    carries = lax.fori_loop(0, jnp.maximum(kb - 1, 0), far_body, tuple(carries))
    out_t = jnp.concatenate([acc / l for (_, l, acc) in carries], axis=0)
    o_ref[0] = out_t.T.astype(o_ref.dtype)


def moba_prompt(q, kbf, vbf, kmean, rel_bias):
    n, s, _ = q.shape
    nb = s // MOBA_BLOCK
    i = jnp.arange(MOBA_BLOCK)
    rel_own = i[None, :] - i[:, None]
    bkt_own = t5_bucket(rel_own).astype(jnp.int32)
    bkt_prev = t5_bucket(rel_own + MOBA_BLOCK).astype(jnp.int32)
    grid_spec = pltpu.PrefetchScalarGridSpec(
        num_scalar_prefetch=1,
        grid=(n, nb),
        in_specs=[pl.BlockSpec((1, MOBA_BLOCK, ATT_W), lambda a, b, *_: (a, b, 0)),
                  pl.BlockSpec((1, s, ATT_W), lambda a, b, *_: (a, 0, 0)),
                  pl.BlockSpec((1, s, ATT_W), lambda a, b, *_: (a, 0, 0)),
                  pl.BlockSpec((1, nb, ATT_W), lambda a, b, *_: (a, 0, 0)),
                  pl.BlockSpec((MOBA_BLOCK, MOBA_BLOCK), lambda a, b, *_: (0, 0)),
                  pl.BlockSpec((MOBA_BLOCK, MOBA_BLOCK), lambda a, b, *_: (0, 0))],
        out_specs=pl.BlockSpec((1, MOBA_BLOCK, ATT_W), lambda a, b, *_: (a, b, 0)),
        scratch_shapes=[pltpu.VMEM((H_A, MOBA_BLOCK, MOBA_BLOCK), F32),
                        pltpu.VMEM((H_A, MOBA_BLOCK, MOBA_BLOCK), F32)],
    )
    return pl.pallas_call(
        _moba_prompt_kernel,
        grid_spec=grid_spec,
        out_shape=jax.ShapeDtypeStruct((n, s, ATT_W), BF16),
        compiler_params=_params("arbitrary", "arbitrary"),
        name="moba_prompt",
    )(rel_bias, q, kbf, vbf, kmean, bkt_own, bkt_prev)


def _moba_sample_kernel(pt_ref, q_ref, knew_ref, vnew_ref, relb_rows_ref, bkt_last_ref, bkt_own_ref,
                        ck_hbm, cv_hbm, o_ref,
                        kbuf, vbuf, kown, vown, logit, prob, means, bias_last, bias_own, sems):
    seq = pl.program_id(0)
    n_seq = pl.num_programs(0)
    n_pages = kbuf.shape[0]
    n_blocks = n_pages // PAGES_PER_BLOCK
    t_new = q_ref.shape[1]
    n_rows = t_new * H_A

    def page_copy(cache, buf, sem_slot, s, j):
        return pltpu.make_async_copy(cache.at[pt_ref[s, j]], buf.at[j], sems.at[sem_slot])

    def start_pages(cache, buf, sem_slot, s):
        for j in range(n_pages):
            page_copy(cache, buf, sem_slot, s, j).start()

    def wait_pages(cache, buf, sem_slot, s):
        for j in range(n_pages):
            page_copy(cache, buf, sem_slot, s, j).wait()

    @pl.when(seq == 0)
    def _():
        start_pages(ck_hbm, kbuf, 0, 0)
        start_pages(cv_hbm, vbuf, 1, 0)
        kown[...] = jnp.zeros_like(kown)
        vown[...] = jnp.zeros_like(vown)
        means[...] = jnp.zeros_like(means)
        relb_rows = relb_rows_ref[...]
        bias_last[...] = _bias_rows_from_buckets(bkt_last_ref[...], relb_rows)
        bias_own[...] = _bias_rows_from_buckets(bkt_own_ref[...], relb_rows)

    q = q_ref[0]
    head_mask = (lax.broadcasted_iota(jnp.int32, (H_A, ATT_W), 1) // HD_A
                 == lax.broadcasted_iota(jnp.int32, (H_A, ATT_W), 0))
    qbd = jnp.concatenate([jnp.where(head_mask, jnp.broadcast_to(q[t:t + 1], (H_A, ATT_W)), 0.0)
                           for t in range(t_new)], axis=0)
    qs = (qbd * (HD_A ** -0.5)).astype(BF16)
    kown[0:t_new, :] = knew_ref[0]
    vown[0:t_new, :] = vnew_ref[0]

    wait_pages(ck_hbm, kbuf, 0, seq)

    blk_lane = lax.broadcasted_iota(jnp.int32, means.shape, 1)

    def k_block(b, carry):
        tot = jnp.zeros((ATT_W, PAGE_SIZE), F32)
        for pg in range(PAGES_PER_BLOCK):
            j = b * PAGES_PER_BLOCK + pg
            kp = kbuf[j]
            tot = tot + kp
            logit[j] = jnp.dot(qs, kp.astype(BF16), preferred_element_type=F32)
        mean_col = jnp.sum(tot, axis=1, keepdims=True) * (1.0 / MOBA_BLOCK)
        means[...] = jnp.where(blk_lane == b, mean_col, means[...])
        return carry

    lax.fori_loop(0, n_blocks, k_block, 0, unroll=2)

    @pl.when(seq + 1 < n_seq)
    def _():
        start_pages(ck_hbm, kbuf, 0, seq + 1)

    scores = jnp.dot(qbd, means[:, 0:n_blocks], precision=HIGHEST, preferred_element_type=F32)
    picks = _top_block_ids(scores, n_blocks, min(MOBA_TOPK, n_blocks))

    far_bias = relb_rows_ref[:, N_BUCKETS - 1:N_BUCKETS]
    own_ok = bkt_own_ref[...] >= 0
    s_own = lax.dot_general(qs, kown[...].astype(BF16), NT_DIMS, preferred_element_type=F32)
    s_own = jnp.where(own_ok, s_own + bias_own[...], NEG_INF)
    mx = s_own
    for j in range(n_pages):
        b, pg = divmod(j, PAGES_PER_BLOCK)
        if b == n_blocks - 1:
            bias = bias_last[:, pg * PAGE_SIZE:(pg + 1) * PAGE_SIZE]
        else:
            bias = far_bias
        s = jnp.where(_picked(picks, b), logit[j] + bias, NEG_INF)
        logit[j] = s
        mx = jnp.maximum(mx, s)
    m = jnp.max(mx, axis=1, keepdims=True)
    p_own = jnp.exp(s_own - m)
    lsum = p_own
    for j in range(n_pages):
        p = jnp.exp(logit[j] - m)
        prob[j] = p.astype(BF16)
        lsum = lsum + p
    l = jnp.sum(lsum, axis=1, keepdims=True)

    wait_pages(cv_hbm, vbuf, 1, seq)

    def v_block(b, acc):
        for pg in range(PAGES_PER_BLOCK):
            j = b * PAGES_PER_BLOCK + pg
            acc = acc + lax.dot_general(prob[j], vbuf[j].astype(BF16), NT_DIMS,
                                        preferred_element_type=F32)
        return acc

    acc = lax.fori_loop(0, n_blocks, v_block, jnp.zeros((n_rows, ATT_W), F32), unroll=4)

    @pl.when(seq + 1 < n_seq)
    def _():
        start_pages(cv_hbm, vbuf, 1, seq + 1)

    acc = acc + jnp.dot(p_own.astype(BF16), vown[...].astype(BF16), preferred_element_type=F32)
    acc = acc / l
    for t in range(t_new):
        rows = acc[t * H_A:(t + 1) * H_A]
        o_ref[0, t:t + 1, :] = jnp.sum(jnp.where(head_mask, rows, 0.0), axis=0, keepdims=True)


def moba_sample(q, k_new, v_new, cache_k, cache_v, page_table, rel_bias):
    n, t, _ = q.shape
    n_pages = page_table.shape[1]
    assert n_pages % PAGES_PER_BLOCK == 0 and PAGES_PER_BLOCK <= n_pages <= PAGE_SIZE * PAGES_PER_BLOCK
    n_rows = t * H_A
    t_of_row = jnp.arange(n_rows) // H_A
    relb_rows = jnp.tile(rel_bias.T, (t, 1))
    bkt_last = t5_bucket(MOBA_BLOCK + t_of_row[:, None] - jnp.arange(MOBA_BLOCK)[None, :]).astype(jnp.int32)
    cols = jnp.arange(PAGE_SIZE)[None, :]
    rel_own = t_of_row[:, None] - cols
    bkt_own = jnp.where((rel_own >= 0) & (cols < t), t5_bucket(rel_own), -1).astype(jnp.int32)
    seq_spec = pl.BlockSpec((1, t, ATT_W), lambda s, *_: (s, 0, 0))
    const = lambda shape: pl.BlockSpec(shape, lambda s, *_: (0,) * len(shape))
    grid_spec = pltpu.PrefetchScalarGridSpec(
        num_scalar_prefetch=1,
        grid=(n,),
        in_specs=[seq_spec, seq_spec, seq_spec,
                  const((n_rows, N_BUCKETS)), const((n_rows, MOBA_BLOCK)), const((n_rows, PAGE_SIZE)),
                  pl.BlockSpec(memory_space=pl.ANY), pl.BlockSpec(memory_space=pl.ANY)],
        out_specs=seq_spec,
        scratch_shapes=[pltpu.VMEM((n_pages, ATT_W, PAGE_SIZE), F32),
                        pltpu.VMEM((n_pages, ATT_W, PAGE_SIZE), F32),
                        pltpu.VMEM((PAGE_SIZE, ATT_W), F32),
                        pltpu.VMEM((PAGE_SIZE, ATT_W), F32),
                        pltpu.VMEM((n_pages, n_rows, PAGE_SIZE), F32),
                        pltpu.VMEM((n_pages, n_rows, PAGE_SIZE), BF16),
                        pltpu.VMEM((ATT_W, PAGE_SIZE), F32),
                        pltpu.VMEM((n_rows, MOBA_BLOCK), F32),
                        pltpu.VMEM((n_rows, PAGE_SIZE), F32),
                        pltpu.SemaphoreType.DMA((2,))],
    )
    return pl.pallas_call(
        _moba_sample_kernel,
        grid_spec=grid_spec,
        out_shape=jax.ShapeDtypeStruct((n, t, ATT_W), F32),
        compiler_params=_params("arbitrary"),
        name="moba_sample",
    )(page_table, q, k_new, v_new, relb_rows, bkt_last, bkt_own, cache_k, cache_v)


def _swap_pairs(x):
    lane = lax.broadcasted_iota(jnp.int32, x.shape, 1)
    width = x.shape[1]
    return jnp.where(lane % 2 == 0, pltpu.roll(x, width - 1, 1), pltpu.roll(x, 1, 1))


def _retention_kernel(q_ref, k_ref, v_ref, g_ref, cos_ref, sin_ref, decay_ref, cross_ref, kw_ref,
                      sdec_ref, ng_ref, nb_ref, st_in_ref, y_ref, st_out_ref, st):
    c = pl.program_id(1)

    @pl.when(c == 0)
    def _():
        st[...] = st_in_ref[0]

    cos = cos_ref[...]
    sin = sin_ref[...]
    q = q_ref[0]
    k = k_ref[0]
    v = v_ref[0]
    g = g_ref[0]
    for h in range(H_R):
        lanes = slice(h * DK_R, (h + 1) * DK_R)
        qh = q[:, lanes]
        kh = k[:, lanes]
        qh = (qh * cos + _swap_pairs(qh) * sin).astype(BF16)
        kh = (kh * cos + _swap_pairs(kh) * sin) * (DK_R ** -0.5)
        vh = v[:, lanes]
        state = st[h]
        scores = lax.dot_general(qh, kh.astype(BF16), NT_DIMS, preferred_element_type=F32) * decay_ref[h]
        inner = jnp.dot(scores.astype(BF16), vh, preferred_element_type=F32)
        cross = jnp.dot(qh, state.astype(BF16), preferred_element_type=F32) * cross_ref[:, lanes]
        kd = (kh * kw_ref[:, lanes]).astype(BF16)
        st[h] = sdec_ref[h] * state + lax.dot_general(kd, vh, TN_DIMS, preferred_element_type=F32)
        o = inner + cross
        o = _layer_norm(o) * ng_ref[:, lanes] + nb_ref[:, lanes]
        y_ref[0, :, lanes] = (_silu(g[:, lanes]) * o).astype(y_ref.dtype)

    @pl.when(c == pl.num_programs(1) - 1)
    def _():
        st_out_ref[0] = st[...]


def _retention_tables(pos, n_valid, chunk):
    theta = 1.0 / (ROPE_BASE ** jnp.linspace(0.0, 1.0, DK_R // 2, dtype=F32))
    ang = pos.astype(F32)[:, None] * theta[None, :]
    cos = jnp.repeat(jnp.cos(ang), 2, axis=1)
    sin = jnp.repeat(jnp.sin(ang), 2, axis=1) * jnp.tile(jnp.array([-1.0, 1.0], F32), DK_R // 2)[None, :]
    log_g = jnp.log1p(-jnp.exp2(-5.0 - jnp.arange(H_R, dtype=F32)))
    i = jnp.arange(chunk, dtype=F32)
    valid = i < n_valid
    diff = i[:, None] - i[None, :]
    decay = jnp.where(diff >= 0, jnp.exp(jnp.maximum(diff, 0.0)[None] * log_g[:, None, None]), 0.0)
    decay = jnp.where(valid[None, :, None] & valid[None, None, :], decay, 0.0)
    cross = jnp.where(valid[:, None], jnp.exp((i + 1.0)[:, None] * log_g[None, :]), 0.0)
    kw = jnp.where(valid[:, None], jnp.exp((n_valid - 1.0 - i)[:, None] * log_g[None, :]), 0.0)
    sdec = jnp.exp(n_valid * log_g)
    return dict(cos=cos, sin=sin, decay=decay,
                cross=jnp.repeat(cross, DV_R, axis=1), kw=jnp.repeat(kw, DK_R, axis=1),
                sdec=jnp.broadcast_to(sdec[:, None, None], (H_R, 1, DV_R)))


def retention(q, k, v, g, state0, tables, norm_g, norm_b):
    n, s, _ = q.shape
    chunk = tables["decay"].shape[1]
    n_chunks = s // chunk
    seq = pl.BlockSpec((1, chunk, RET_QK_W), lambda a, c: (a, c, 0))
    per_chunk = pl.BlockSpec((chunk, DK_R), lambda a, c: (c, 0))
    per_row = pl.BlockSpec((chunk, RET_QK_W), lambda a, c: (0, 0))
    st_spec = pl.BlockSpec((1, H_R, DK_R, DV_R), lambda a, c: (a, 0, 0, 0))
    vec = pl.BlockSpec((1, RET_V_W), lambda a, c: (0, 0))
    return pl.pallas_call(
        _retention_kernel,
        grid=(n, n_chunks),
        in_specs=[seq, seq, seq, seq, per_chunk, per_chunk,
                  pl.BlockSpec((H_R, chunk, chunk), lambda a, c: (0, 0, 0)),
                  per_row, per_row,
                  pl.BlockSpec((H_R, 1, DV_R), lambda a, c: (0, 0, 0)),
                  vec, vec, st_spec],
        out_specs=[seq, st_spec],
        out_shape=[jax.ShapeDtypeStruct((n, s, RET_V_W), BF16),
                   jax.ShapeDtypeStruct((n, H_R, DK_R, DV_R), F32)],
        scratch_shapes=[pltpu.VMEM((H_R, DK_R, DV_R), F32)],
        compiler_params=_params("arbitrary", "arbitrary"),
        name="retention",
    )(q, k, v, g, tables["cos"], tables["sin"], tables["decay"], tables["cross"], tables["kw"],
      tables["sdec"], norm_g.reshape(1, -1), norm_b.reshape(1, -1), state0)


def _mix_kernel(att_ref, ret_ref, ga_ref, gb_ref, x_ref, gate1_ref, shift2_ref, scale2_ref,
                wa_ref, wb_ref, wo_ref, g1_ref, b1_ref, wr_ref, x1_ref, h2_ref, h2w_ref, logit_ref):
    ya = jnp.dot(att_ref[...], wa_ref[...], preferred_element_type=F32)
    yb = jnp.dot(ret_ref[...], wb_ref[...], preferred_element_type=F32)
    mixed_in = jax.nn.sigmoid(ga_ref[...]) * ya + jax.nn.sigmoid(gb_ref[...]) * yb
    mixed = jnp.dot(mixed_in.astype(BF16), wo_ref[...], preferred_element_type=F32)
    x1 = _layer_norm(ALPHA * x_ref[...] + gate1_ref[0] * mixed) * g1_ref[...] + b1_ref[...]
    x1_ref[...] = x1
    h2 = _layer_norm(x1) * (1.0 + scale2_ref[0]) + shift2_ref[0]
    h2_bf = h2.astype(BF16)
    h2_ref[...] = h2_bf
    h2w_ref[...] = _pack_bf16_halves(h2_bf)
    logit_ref[...] = jnp.dot(h2, wr_ref[...], precision=HIGHEST, preferred_element_type=F32)


def mix_out(att, ret, ga, gb, x, gate1, shift2, scale2, wa, wb, wo, ln_g, ln_b, w_router, tiles_per_group):
    m = x.shape[0]
    tm = ROW_TILE
    r = gate1.shape[1]
    row = lambda w: pl.BlockSpec((tm, w), lambda i: (i, 0))
    mod = pl.BlockSpec((1, r, D_MODEL), lambda i: (i // tiles_per_group, 0, 0))
    return pl.pallas_call(
        _mix_kernel,
        grid=(m // tm,),
        in_specs=[row(ATT_W), row(RET_V_W), row(D_MODEL), row(D_MODEL), row(D_MODEL), mod, mod, mod,
                  _resident((ATT_W, D_MODEL)), _resident((RET_V_W, D_MODEL)), _resident((D_MODEL, D_MODEL)),
                  _resident((1, D_MODEL)), _resident((1, D_MODEL)), _resident((D_MODEL, N_EXPERTS))],
        out_specs=[row(D_MODEL), row(D_MODEL), row(D_MODEL // 2), row(N_EXPERTS)],
        out_shape=[jax.ShapeDtypeStruct((m, D_MODEL), F32), jax.ShapeDtypeStruct((m, D_MODEL), BF16),
                   jax.ShapeDtypeStruct((m, D_MODEL // 2), jnp.uint32),
                   jax.ShapeDtypeStruct((m, N_EXPERTS), F32)],
        compiler_params=_params("arbitrary"),
        name="mix_out",
    )(att, ret, ga, gb, x, gate1, shift2, scale2, wa, wb, wo, ln_g.reshape(1, -1), ln_b.reshape(1, -1),
      w_router)


def _first_max(x, ids, sentinel):
    best = jnp.max(x, axis=0, keepdims=True)
    first = jnp.min(jnp.where(x == best, ids, sentinel), axis=0, keepdims=True)
    return best, first


def _route_kernel(logit_ref, bias_ref, idx_ref, gate_ref):
    s = jax.nn.sigmoid(logit_ref[...].T)
    sb = s + bias_ref[...]
    tm = s.shape[1]
    gid = lax.broadcasted_iota(jnp.int32, (GROUP_SIZE, tm), 0)
    grp_scores = []
    for gi in range(N_GROUPS):
        grp = sb[gi * GROUP_SIZE:(gi + 1) * GROUP_SIZE]
        top1, first = _first_max(grp, gid, GROUP_SIZE)
        top2 = jnp.max(jnp.where(gid == first, NEG_INF, grp), axis=0, keepdims=True)
        grp_scores.append(top1 + top2)
    gs = jnp.concatenate(grp_scores, axis=0)
    gids = lax.broadcasted_iota(jnp.int32, (N_GROUPS, tm), 0)
    gmask = jnp.zeros((N_GROUPS, tm), jnp.bool_)
    for _ in range(TOPK_GROUPS):
        _, first = _first_max(gs, gids, N_GROUPS)
        hit = gids == first
        gmask = jnp.logical_or(gmask, hit)
        gs = jnp.where(hit, NEG_INF, gs)
    masked = jnp.concatenate(
        [jnp.where(gmask[gi:gi + 1], sb[gi * GROUP_SIZE:(gi + 1) * GROUP_SIZE], NEG_INF)
         for gi in range(N_GROUPS)], axis=0)
    eids = lax.broadcasted_iota(jnp.int32, (N_EXPERTS, tm), 0)
    picks, weights = [], []
    for _ in range(TOP_K):
        _, first = _first_max(masked, eids, N_EXPERTS)
        hit = eids == first
        picks.append(first)
        weights.append(jnp.sum(jnp.where(hit, s, 0.0), axis=0, keepdims=True))
        masked = jnp.where(hit, NEG_INF, masked)
    w = jnp.concatenate(weights, axis=0)
    idx_ref[...] = jnp.concatenate(picks, axis=0)
    gate_ref[...] = w / jnp.sum(w, axis=0, keepdims=True) * ROUTED_SCALE


def route(logits, router_bias):
    m = logits.shape[0]
    tm = ROW_TILE
    return pl.pallas_call(
        _route_kernel,
        grid=(m // tm,),
        in_specs=[pl.BlockSpec((tm, N_EXPERTS), lambda i: (i, 0)),
                  pl.BlockSpec((N_EXPERTS, 1), lambda i: (0, 0))],
        out_specs=[pl.BlockSpec((TOP_K, tm), lambda i: (0, i)), pl.BlockSpec((TOP_K, tm), lambda i: (0, i))],
        out_shape=[jax.ShapeDtypeStruct((TOP_K, m), jnp.int32), jax.ShapeDtypeStruct((TOP_K, m), F32)],
        compiler_params=_params("arbitrary"),
        name="route",
    )(logits, router_bias.reshape(-1, 1))


def _expert_kernel(blk_e_ref, n_used_ref, blk_pos_ref, tok_ref, wg_ref, wu_ref, wd_ref, hw_ref, y_ref, xw):
    i = pl.program_id(0)
    n_used = n_used_ref[0]
    tm = xw.shape[0]

    @pl.when(i < n_used)
    def _():
        off = blk_pos_ref[i] & (tm - 1)
        for r in range(tm):
            xw[r:r + 1, :] = hw_ref[pl.ds(tok_ref[0, 0, off + r], 1), :]
        x = _unpack_bf16_halves(xw[...])
        a = (_silu(jnp.dot(x, wg_ref[0].astype(BF16), preferred_element_type=F32))
             * jnp.dot(x, wu_ref[0].astype(BF16), preferred_element_type=F32))
        y_ref[...] = jnp.dot(a.astype(BF16), wd_ref[0].astype(BF16), preferred_element_type=F32)

    @pl.when(i >= n_used)
    def _():
        y_ref[...] = jnp.zeros_like(y_ref)


def routed_experts(h_words, tok_windows, blk_pos, blk_e, n_used, w_gate, w_up, w_down):
    n_blocks = blk_e.shape[0]
    tm = tok_windows.shape[2] // 2
    assert tm & (tm - 1) == 0
    m, half = h_words.shape
    weights = lambda shape: pl.BlockSpec(shape, lambda i, be, nu, bp: (be[i], 0, 0))
    grid_spec = pltpu.PrefetchScalarGridSpec(
        num_scalar_prefetch=3,
        grid=(n_blocks,),
        in_specs=[pl.BlockSpec((1, 1, 2 * tm), lambda i, be, nu, bp: (bp[i] // tm, 0, 0),
                               memory_space=pltpu.SMEM),
                  weights((1, D_MODEL, D_EXPERT)), weights((1, D_MODEL, D_EXPERT)),
                  weights((1, D_EXPERT, D_MODEL)),
                  _resident((m, half))],
        out_specs=pl.BlockSpec((tm, D_MODEL), lambda i, be, nu, bp: (i, 0)),
        scratch_shapes=[pltpu.VMEM((tm, half), jnp.uint32)],
    )
    return pl.pallas_call(
        _expert_kernel,
        grid_spec=grid_spec,
        out_shape=jax.ShapeDtypeStruct((n_blocks * tm, D_MODEL), F32),
        compiler_params=_params("arbitrary"),
        name="routed_experts",
    )(blk_e, n_used, blk_pos, tok_windows, w_gate, w_up, w_down, h_words)


def _final_kernel(dest_ref, dest_next_ref, x1_ref, h2_ref, gate_ref, gate2_ref, wsg_ref, wsu_ref, wsd_ref,
                  g2_ref, b2_ref, ys_hbm, y_ref, rbuf, sems):
    i = pl.program_id(0)
    n_steps = pl.num_programs(0)
    tm = x1_ref.shape[0]
    n_gather = TOP_K * tm
    slot = i % 2

    def row_copy(dest_smem, r, to_slot):
        return pltpu.make_async_copy(ys_hbm.at[pl.ds(dest_smem[0, 0, r], 1)],
                                     rbuf.at[to_slot, pl.ds(r, 1)], sems.at[to_slot])

    def start_rows(dest_smem, to_slot):
        unroll = 16

        def body(g, c):
            for u in range(unroll):
                row_copy(dest_smem, g * unroll + u, to_slot).start(priority=u % 2)
            return c
        lax.fori_loop(0, n_gather // unroll, body, 0)

    def wait_rows(to_slot):
        pltpu.make_async_copy(ys_hbm.at[pl.ds(0, n_gather)], rbuf.at[to_slot], sems.at[to_slot]).wait()

    @pl.when(i == 0)
    def _():
        start_rows(dest_ref, 0)

    @pl.when(i + 1 < n_steps)
    def _():
        start_rows(dest_next_ref, 1 - slot)

    h = h2_ref[...]
    a = (_silu(jnp.dot(h, wsg_ref[...], preferred_element_type=F32))
         * jnp.dot(h, wsu_ref[...], preferred_element_type=F32))
    f = jnp.dot(a.astype(BF16), wsd_ref[...], preferred_element_type=F32)
    wait_rows(slot)
    gate = gate_ref[...]
    routed = jnp.zeros((tm, D_MODEL), F32)
    for k in range(TOP_K):
        routed = routed + rbuf[slot, k * tm:(k + 1) * tm, :] * gate[:, k:k + 1]
    f = routed + f
    y_ref[...] = _layer_norm(ALPHA * x1_ref[...] + gate2_ref[...] * f) * g2_ref[...] + b2_ref[...]


def final_out(x1, h2, gate, gate2_rows, dest, ys, wsg, wsu, wsd, ln_g, ln_b):
    m = x1.shape[0]
    tm = COMBINE_TILE
    n_steps = m // tm
    row = lambda w: pl.BlockSpec((tm, w), lambda i: (i, 0))
    dest_spec = lambda f: pl.BlockSpec((1, 1, TOP_K * tm), lambda i: (f(i), 0, 0), memory_space=pltpu.SMEM)
    return pl.pallas_call(
        _final_kernel,
        grid=(n_steps,),
        in_specs=[dest_spec(lambda i: i), dest_spec(lambda i: jnp.minimum(i + 1, n_steps - 1)),
                  row(D_MODEL), row(D_MODEL), row(TOP_K), row(D_MODEL),
                  _resident((D_MODEL, D_SHARED)), _resident((D_MODEL, D_SHARED)), _resident((D_SHARED, D_MODEL)),
                  _resident((1, D_MODEL)), _resident((1, D_MODEL)),
                  pl.BlockSpec(memory_space=pl.ANY)],
        out_specs=row(D_MODEL),
        out_shape=jax.ShapeDtypeStruct((m, D_MODEL), F32),
        scratch_shapes=[pltpu.VMEM((2, TOP_K * tm, D_MODEL), F32), pltpu.SemaphoreType.DMA((2,))],
        compiler_params=_params("arbitrary"),
        name="final_out",
    )(dest, dest, x1, h2, gate, gate2_rows, wsg, wsu, wsd, ln_g.reshape(1, -1), ln_b.reshape(1, -1), ys)


def _dispatch(idx_t):
    m = idx_t.shape[1]
    bm = MOE_ROW_TILE
    n_pairs = m * TOP_K
    experts = jnp.arange(N_EXPERTS, dtype=jnp.int32)
    pair_ids = jnp.arange(n_pairs, dtype=jnp.int32)
    flat_e = idx_t.T.reshape(-1)
    se, order = lax.sort((flat_e, pair_ids), num_keys=1, is_stable=True)
    counts = jnp.sum((flat_e[:, None] == experts[None, :]).astype(jnp.int32), axis=0)
    padded = (counts + bm - 1) // bm * bm
    pad_end = jnp.cumsum(padded)
    pad_start = pad_end - padded
    start = jnp.cumsum(counts) - counts
    shift = (pad_start - start).astype(jnp.int32)
    dest_sorted = pair_ids + jnp.sum(jnp.where(se[:, None] == experts[None, :], shift[None, :], 0), axis=1)
    n_blocks = -(-(n_pairs + N_EXPERTS * (bm - 1)) // bm)
    _, dest = lax.sort((order, dest_sorted), num_keys=1)
    blk_row0 = jnp.arange(n_blocks, dtype=jnp.int32) * bm
    blk_e = jnp.minimum(jnp.searchsorted(pad_end, blk_row0, side='right'), N_EXPERTS - 1).astype(jnp.int32)
    n_used = (pad_end[-1] // bm).astype(jnp.int32).reshape(1)
    blk_pos = jnp.where(jnp.arange(n_blocks) < n_used[0], blk_row0 - shift[blk_e], 0).astype(jnp.int32)
    n_win = -(-n_pairs // bm)
    halves = jnp.pad(order // TOP_K, (0, (n_win + 1) * bm - n_pairs)).reshape(n_win + 1, bm)
    tok_windows = jnp.concatenate([halves[:-1], halves[1:]], axis=1).reshape(n_win, 1, 2 * bm)
    tc = COMBINE_TILE
    dest_tiles = dest.reshape(m // tc, tc, TOP_K).transpose(0, 2, 1).reshape(m // tc, 1, TOP_K * tc)
    return tok_windows, blk_pos, blk_e, n_used, dest_tiles


def _expand_rows(v, reps):
    return jnp.repeat(v, reps, axis=0).reshape(-1, ROW_TILE, v.shape[-1])


def kernel(x_prompt, x_sample, cache_k, cache_v, state_ret, page_table, c_prompt, c_sample, w_ada, b_ada, w_in, rel_bias, ret_norm_g, ret_norm_b, w_branch_a, w_branch_b, w_out, ln1_g, ln1_b, w_router, router_bias, w_gate, w_up, w_down, w_shared_gate, w_shared_up, w_shared_down, ln2_g, ln2_b):
    assert DEPTH == 1
    l = 0
    nb, s, d = x_prompt.shape
    ns, t, _ = x_sample.shape
    past_len = page_table.shape[1] * PAGE_SIZE
    mp, ms = nb * s, ns * t

    c_all = jnp.concatenate([c_prompt, c_sample], axis=0)
    n_c = c_all.shape[0]
    c_all = jnp.pad(c_all, ((0, -n_c % 8), (0, 0)))
    mod = ada_modulation(c_all, w_ada[l], b_ada[l])
    mods_p = [mod[:nb, i * d:(i + 1) * d].reshape(nb, 1, d) for i in range(6)]
    mods_s = [_expand_rows(mod[nb:nb + ns, i * d:(i + 1) * d], t) for i in range(6)]
    tiles_p = s // ROW_TILE

    w_in_bf = w_in[l].astype(BF16)
    pp = in_proj(x_prompt.reshape(mp, d), mods_p[0], mods_p[1], w_in_bf, tiles_p)
    ps = in_proj(x_sample.reshape(ms, d), mods_s[0], mods_s[1], w_in_bf, 1)

    att_p = moba_prompt(pp["qa"].reshape(nb, s, ATT_W), pp["kbf"].reshape(nb, s, ATT_W),
                        pp["vbf"].reshape(nb, s, ATT_W), pp["kmean"].reshape(nb, s // MOBA_BLOCK, ATT_W),
                        rel_bias)
    tab_p = _retention_tables(jnp.arange(s), float(RET_CHUNK), RET_CHUNK)
    ret_p, st_p = retention(pp["qr"].reshape(nb, s, -1), pp["kr"].reshape(nb, s, -1),
                            pp["vr"].reshape(nb, s, -1), pp["gr"].reshape(nb, s, -1),
                            jnp.zeros((nb, H_R, DK_R, DV_R), F32), tab_p, ret_norm_g[l], ret_norm_b[l])

    n_pool = cache_k.shape[1]
    page_major = lambda c: jnp.transpose(c[l], (0, 2, 3, 1)).reshape(n_pool, ATT_W, PAGE_SIZE)
    att_s = moba_sample(ps["qa"].reshape(ns, t, ATT_W), ps["ka"].reshape(ns, t, ATT_W),
                        ps["va"].reshape(ns, t, ATT_W), page_major(cache_k), page_major(cache_v),
                        page_table, rel_bias)
    pad_t = lambda a: jnp.pad(a.reshape(ns, t, -1), ((0, 0), (0, RET_CHUNK - t), (0, 0)))
    pos_s = past_len + jnp.arange(RET_CHUNK)
    tab_s = _retention_tables(pos_s, float(t), RET_CHUNK)
    ret_s, st_s = retention(pad_t(ps["qr"]), pad_t(ps["kr"]), pad_t(ps["vr"]), pad_t(ps["gr"]),
                            state_ret[l], tab_s, ret_norm_g[l], ret_norm_b[l])
    ret_s = ret_s[:, :t].reshape(ms, RET_V_W)

    wa, wb, wo = w_branch_a[l].astype(BF16), w_branch_b[l].astype(BF16), w_out[l].astype(BF16)
    x1_p, h2_p, hw_p, lg_p = mix_out(att_p.reshape(mp, ATT_W), ret_p.reshape(mp, RET_V_W), pp["ga"],
                                     pp["gb"], x_prompt.reshape(mp, d), mods_p[2], mods_p[3], mods_p[4],
                                     wa, wb, wo, ln1_g[l], ln1_b[l], w_router[l], tiles_p)
    x1_s, h2_s, hw_s, lg_s = mix_out(att_s.reshape(ms, ATT_W).astype(BF16), ret_s, ps["ga"], ps["gb"],
                                     x_sample.reshape(ms, d), mods_s[2], mods_s[3], mods_s[4],
                                     wa, wb, wo, ln1_g[l], ln1_b[l], w_router[l], 1)

    x1 = jnp.concatenate([x1_p, x1_s], axis=0)
    h2 = jnp.concatenate([h2_p, h2_s], axis=0)
    idx_t, gate_t = route(jnp.concatenate([lg_p, lg_s], axis=0), router_bias[l])
    tok_windows, blk_pos, blk_e, n_used, dest_tiles = _dispatch(idx_t)
    ys = routed_experts(jnp.concatenate([hw_p, hw_s], axis=0), tok_windows, blk_pos, blk_e, n_used,
                        w_gate[l], w_up[l], w_down[l])
    gate2_rows = jnp.concatenate([jnp.broadcast_to(mods_p[5], (nb, s, d)).reshape(mp, d),
                                  mods_s[5].reshape(ms, d)], axis=0)
    wsg, wsu, wsd = (w_shared_gate[l].astype(BF16), w_shared_up[l].astype(BF16),
                     w_shared_down[l].astype(BF16))
    y = final_out(x1, h2, gate_t.T, gate2_rows, dest_tiles, ys, wsg, wsu, wsd, ln2_g[l], ln2_b[l])

    return (y[:mp].reshape(nb, s, d), y[mp:].reshape(ns, t, d),
            pp["ka"].reshape(1, nb, s, H_A, HD_A), pp["va"].reshape(1, nb, s, H_A, HD_A), st_p[None],
            ps["ka"].reshape(1, ns, t, H_A, HD_A), ps["va"].reshape(1, ns, t, H_A, HD_A), st_s[None])
```
